```python
import jax, jax.numpy as jnp
from jax import lax
import numpy as np

D_MODEL = 1024
BATCH = 32
SEQ = 2048
DEPTH = 1

MEM_LEN = 256
GDN_HEADS = 8
GDN_DK = 128
GDN_DV = 128
GDN_CONV = 4
GDN_CHUNK = 64
GDN_QK_W = GDN_HEADS * GDN_DK
GDN_V_W = GDN_HEADS * GDN_DV
GDN_QKV_W = 2 * GDN_QK_W + GDN_V_W
SB_HEADS = 8
SB_DH = 128
SB_W = SB_HEADS * SB_DH
SB_BLOCK = 128
X_HEADS = 4
X_DH = D_MODEL // X_HEADS
D_FF = 2816
FFN_CONV = 3
EPS = 1e-6
REST_WIDTHS = (GDN_HEADS, GDN_HEADS, GDN_V_W, SB_W, SB_W, SB_W, D_MODEL, D_MODEL)
IN_W = GDN_QKV_W + sum(REST_WIDTHS)

kernel_name = "hybrid_gdn_stickbreak_memxattn_convffn"


def rmsnorm(x, g):
    xf = x.astype(jnp.float32)
    y = xf * lax.rsqrt(jnp.mean(xf * xf, axis=-1, keepdims=True) + EPS)
    return (y * g.astype(jnp.float32)).astype(x.dtype)


def l2norm(x):
    return x * lax.rsqrt(jnp.sum(x * x, axis=-1, keepdims=True) + EPS)


def causal_dwconv(x, w):
    k = w.shape[0]
    return lax.conv_general_dilated(
        x, w[:, None, :].astype(x.dtype), window_strides=(1,), padding=[(k - 1, 0)],
        dimension_numbers=("NWC", "WIO", "NWC"), feature_group_count=x.shape[-1])


def split_heads(t, h):
    return t.reshape(t.shape[:-1] + (h, t.shape[-1] // h))


def to_chunks(t):
    b, s, h = t.shape[:3]
    t = jnp.moveaxis(t, 2, 1)
    return t.reshape((b, h, s // GDN_CHUNK, GDN_CHUNK) + t.shape[3:])


def gated_deltanet(q, k, v, a, bgate, z, a_log, dt_bias, out_norm):
    b, s, h, _ = q.shape
    f32 = jnp.float32
    q = l2norm(q.astype(f32)) * (GDN_DK ** -0.5)
    k = l2norm(k.astype(f32))
    v = v.astype(f32)
    beta = jax.nn.sigmoid(bgate.astype(f32))
    g = -jnp.exp(a_log.astype(f32)) * jax.nn.softplus(a.astype(f32) + dt_bias.astype(f32))

    qc, kc, vc = to_chunks(q), to_chunks(k), to_chunks(v)
    betac, gc = to_chunks(beta), jnp.cumsum(to_chunks(g), axis=-1)

    idx = jnp.arange(GDN_CHUNK)
    incl = idx[:, None] >= idx[None, :]
    strict = idx[:, None] > idx[None, :]
    decay = jnp.exp(jnp.where(incl, gc[..., :, None] - gc[..., None, :], -jnp.inf))

    kk = jnp.einsum("bhncd,bhnmd->bhncm", kc, kc)
    m_strict = jnp.where(strict, betac[..., :, None] * kk * decay, 0.0)
    rhs = jnp.concatenate([vc * betac[..., None],
                           kc * (betac * jnp.exp(gc))[..., None]], axis=-1)
    sol = lax.linalg.triangular_solve(m_strict, rhs, left_side=True, lower=True,
                                      unit_diagonal=True)
    u, w = sol[..., :GDN_DV], sol[..., GDN_DV:]

    qk = jnp.einsum("bhncd,bhnmd->bhncm", qc, kc) * decay
    q_dec = qc * jnp.exp(gc)[..., None]
    k_dec = kc * jnp.exp(gc[..., -1:] - gc)[..., None]
    chunk_decay = jnp.exp(gc[..., -1])

    def step(state, inp):
        u_n, w_n, qk_n, qd_n, kd_n, cd_n = inp
        v_new = u_n - jnp.einsum("bhck,bhkv->bhcv", w_n, state)
        o_n = (jnp.einsum("bhck,bhkv->bhcv", qd_n, state)
               + jnp.einsum("bhcm,bhmv->bhcv", qk_n, v_new))
        state = (state * cd_n[..., None, None]
                 + jnp.einsum("bhck,bhcv->bhkv", kd_n, v_new))
        return state, o_n

    xs = tuple(jnp.moveaxis(t, 2, 0) for t in (u, w, qk, q_dec, k_dec, chunk_decay))
    state0 = jnp.zeros((b, h, GDN_DK, GDN_DV), f32)
    _, o = lax.scan(step, state0, xs)
    o = jnp.transpose(o, (1, 0, 3, 2, 4)).reshape(b, s, h, GDN_DV)
    o = rmsnorm(o, out_norm) * jax.nn.silu(z.astype(f32))
    return o.reshape(b, s, h * GDN_DV)


def stick_breaking_attention(q, k, v):
    b, s, h, d = q.shape
    q = q * (d ** -0.5)
    outs = []
    for blk in range(s // SB_BLOCK):
        t0 = blk * SB_BLOCK
        end = t0 + SB_BLOCK
        z = jnp.einsum("bthd,bshd->bhts", q[:, t0:end], k[:, :end]).astype(jnp.float32)
        t_idx = t0 + jnp.arange(SB_BLOCK)
        s_idx = jnp.arange(end)
        causal = s_idx[None, :] < t_idx[:, None]
        log_fail = jnp.where(causal, jax.nn.log_sigmoid(-z), 0.0)
        suffix = lax.cumsum(log_fail, axis=3, reverse=True)
        suffix_excl = jnp.pad(suffix[..., 1:], ((0, 0), (0, 0), (0, 0), (0, 1)))
        weights = jnp.where(causal, jnp.exp(jax.nn.log_sigmoid(z) + suffix_excl), 0.0)
        outs.append(jnp.einsum("bhts,bshd->bthd", weights.astype(v.dtype), v[:, :end]))
    return jnp.concatenate(outs, axis=1).reshape(b, s, h * d)


def hybrid_mixer(x, norm_mix, w_in, conv_gdn, a_log, dt_bias, gdn_out_norm,
                 w_proj_gdn, w_proj_sb, w_out):
    xn = rmsnorm(x, norm_mix)
    proj = xn @ w_in
    qkv = jax.nn.silu(causal_dwconv(proj[..., :GDN_QKV_W], conv_gdn))
    gq = split_heads(qkv[..., :GDN_QK_W], GDN_HEADS)
    gk = split_heads(qkv[..., GDN_QK_W:2 * GDN_QK_W], GDN_HEADS)
    gv = split_heads(qkv[..., 2 * GDN_QK_W:], GDN_HEADS)
    splits = tuple(int(i) for i in np.cumsum(REST_WIDTHS)[:-1])
    a, bg, z, sq, sk, sv, gate_a, gate_b = jnp.split(proj[..., GDN_QKV_W:], splits, axis=-1)
    o_a = gated_deltanet(gq, gk, gv, a, bg, split_heads(z, GDN_HEADS),
                         a_log, dt_bias, gdn_out_norm).astype(x.dtype)
    o_b = stick_breaking_attention(split_heads(sq, SB_HEADS), split_heads(sk, SB_HEADS),
                                   split_heads(sv, SB_HEADS))
    merged = (jax.nn.sigmoid(gate_a) * (o_a @ w_proj_gdn)
              + jax.nn.sigmoid(gate_b) * (o_b @ w_proj_sb))
    return merged @ w_out


def memory_cross_attention(h, mem, norm_x, norm_mem, w_xq, w_xkv, xq_norm, xk_norm, w_xo):
    b, s, _ = h.shape
    hn = rmsnorm(h, norm_x)
    mn = rmsnorm(mem, norm_mem)
    q = rmsnorm(split_heads(hn @ w_xq, X_HEADS), xq_norm)
    kv = mn @ w_xkv
    k = rmsnorm(split_heads(kv[..., :D_MODEL], X_HEADS), xk_norm)
    v = split_heads(kv[..., D_MODEL:], X_HEADS)
    scores = jnp.einsum("bshd,bmhd->bhsm", q, k).astype(jnp.float32) * (X_DH ** -0.5)
    p = jax.nn.softmax(scores, axis=-1)
    o = jnp.einsum("bhsm,bmhd->bshd", p.astype(v.dtype), v).reshape(b, s, D_MODEL)
    return o @ w_xo


def conv_gated_mlp(h, norm_ffn, w_up, conv_ffn, w_down):
    hn = rmsnorm(h, norm_ffn)
    u = causal_dwconv(hn @ w_up, conv_ffn)
    return (jax.nn.silu(u[..., :D_FF]) * u[..., D_FF:]) @ w_down


def setup_inputs(seed: int = 0) -> dict:
    key = jax.random.key(seed)
    ks = jax.random.split(key, 24)
    L = DEPTH

    def dense(k, fan_in, fan_out):
        return jax.random.normal(k, (L, fan_in, fan_out), jnp.float32) * fan_in ** -0.5

    def gain(k, n):
        return 1.0 + 0.02 * jax.random.normal(k, (L, n), jnp.float32)

    dt = jnp.exp(jax.random.uniform(ks[5], (L, GDN_HEADS), jnp.float32,
                                    np.log(1e-3), np.log(1e-1)))
    return {
        "x": jax.random.normal(ks[0], (BATCH, SEQ, D_MODEL), jnp.float32),
        "mem": jax.random.normal(ks[1], (BATCH, MEM_LEN, D_MODEL), jnp.float32),
        "norm_mix": gain(ks[2], D_MODEL),
        "w_in": dense(ks[3], D_MODEL, IN_W),
        "conv_gdn": jax.random.normal(ks[4], (L, GDN_CONV, GDN_QKV_W), jnp.float32) * GDN_CONV ** -0.5,
        "a_log": jnp.log(jax.random.uniform(ks[6], (L, GDN_HEADS), jnp.float32, 1.0, 16.0)),
        "dt_bias": dt + jnp.log(-jnp.expm1(-dt)),
        "gdn_out_norm": gain(ks[7], GDN_DV),
        "w_proj_gdn": dense(ks[8], GDN_V_W, D_MODEL),
        "w_proj_sb": dense(ks[9], SB_W, D_MODEL),
        "w_out": dense(ks[10], D_MODEL, D_MODEL),
        "norm_x": gain(ks[11], D_MODEL),
        "norm_mem": gain(ks[12], D_MODEL),
        "w_xq": dense(ks[13], D_MODEL, D_MODEL),
        "w_xkv": dense(ks[14], D_MODEL, 2 * D_MODEL),
        "xq_norm": gain(ks[15], X_DH),
        "xk_norm": gain(ks[16], X_DH),
        "w_xo": dense(ks[17], D_MODEL, D_MODEL),
        "norm_ffn": gain(ks[18], D_MODEL),
        "w_up": dense(ks[19], D_MODEL, 2 * D_FF),
        "conv_ffn": jax.random.normal(ks[20], (L, FFN_CONV, 2 * D_FF), jnp.float32) * FFN_CONV ** -0.5,
        "w_down": dense(ks[21], D_FF, D_MODEL),
    }


def reference(x, mem, norm_mix, w_in, conv_gdn, a_log, dt_bias, gdn_out_norm,
              w_proj_gdn, w_proj_sb, w_out, norm_x, norm_mem, w_xq, w_xkv,
              xq_norm, xk_norm, w_xo, norm_ffn, w_up, conv_ffn, w_down):
    h = x
    for l in range(DEPTH):
        h = h + hybrid_mixer(h, norm_mix[l], w_in[l], conv_gdn[l], a_log[l], dt_bias[l],
                             gdn_out_norm[l], w_proj_gdn[l], w_proj_sb[l], w_out[l])
        h = h + memory_cross_attention(h, mem, norm_x[l], norm_mem[l], w_xq[l], w_xkv[l],
                                       xq_norm[l], xk_norm[l], w_xo[l])
        h = h + conv_gated_mlp(h, norm_ffn[l], w_up[l], conv_ffn[l], w_down[l])
    return h
```

```python
import functools

import jax
import jax.numpy as jnp
from jax import lax
from jax.experimental import pallas as pl
from jax.experimental.pallas import tpu as pltpu

F32 = jnp.float32
BF = jnp.bfloat16

D_MODEL = 1024
HEADS = 8
HEAD_DIM = 128
CHUNK = 64
TILE = 128
GDN_CONV = 4
X_HEADS = 4
X_DH = D_MODEL // X_HEADS
D_FF = 2816
FFN_CONV = 3
FFN_FC = 256
FFN_HALO = 16
EPS = 1e-6
VMEM_LIMIT = 56 * 1024 * 1024

_GQ, _GK, _GV, _GZ, _SQ, _SK, _SV = (i * HEADS for i in range(7))
N_HEAD_BLOCKS = 7 * HEADS


def _dot(a, b):
    return jnp.dot(a, b, preferred_element_type=F32)


def _dot_nt(a, b):
    return lax.dot_general(a, b, (((1,), (1,)), ((), ())), preferred_element_type=F32)


def _split(a):
    hi = a.astype(BF)
    lo = (a - hi.astype(F32)).astype(BF)
    return hi, lo


def _dot_x3(a, b):
    ah, al = _split(a)
    bh, bl = _split(b)
    return _dot(ah, bh) + (_dot(ah, bl) + _dot(al, bh))


def _rms(x, g):
    return x * lax.rsqrt(jnp.mean(x * x, axis=-1, keepdims=True) + EPS) * g


def _softplus(x):
    return jnp.maximum(x, 0.0) + jnp.log1p(jnp.exp(-jnp.abs(x)))


def _sigmoid(x):
    return 1.0 / (1.0 + jnp.exp(-x))


def _params(*sem):
    return pltpu.CompilerParams(dimension_semantics=sem, vmem_limit_bytes=VMEM_LIMIT)


def _const_spec(shape):
    n = len(shape)
    return pl.BlockSpec(shape, lambda *_: (0,) * n, pipeline_mode=pl.Buffered(1))


def _in_proj_body(x_ref, g_ref, w_ref, wab_ref, heads_ref, xn_ref, ab_ref):
    @pl.when(pl.program_id(1) == 0)
    def _():
        xn = _rms(x_ref[...], g_ref[...]).astype(BF)
        xn_ref[...] = xn
        ab_ref[...] = _dot(xn, wab_ref[...])

    res = _dot(xn_ref[...], w_ref[...])
    for c in range(HEADS):
        heads_ref[c] = res[:, c * HEAD_DIM:(c + 1) * HEAD_DIM].astype(BF)


def _in_proj(x2, g, w_heads, w_ab, tm):
    t = x2.shape[0]
    nj = N_HEAD_BLOCKS // HEADS
    wn = HEADS * HEAD_DIM
    return pl.pallas_call(
        _in_proj_body,
        grid=(t // tm, nj),
        in_specs=[
            pl.BlockSpec((tm, D_MODEL), lambda i, j: (i, 0)),
            _const_spec((1, D_MODEL)),
            pl.BlockSpec((D_MODEL, wn), lambda i, j: (0, j)),
            _const_spec((D_MODEL, HEAD_DIM)),
        ],
        out_specs=[
            pl.BlockSpec((HEADS, tm, HEAD_DIM), lambda i, j: (j, i, 0)),
            pl.BlockSpec((tm, D_MODEL), lambda i, j: (i, 0)),
            pl.BlockSpec((tm, HEAD_DIM), lambda i, j: (i, 0)),
        ],
        out_shape=[
            jax.ShapeDtypeStruct((N_HEAD_BLOCKS, t, HEAD_DIM), BF),
            jax.ShapeDtypeStruct((t, D_MODEL), BF),
            jax.ShapeDtypeStruct((t, HEAD_DIM), F32),
        ],
        compiler_params=_params("parallel", "arbitrary"),
        name="in_proj",
    )(x2, g, w_heads, w_ab)


def _gdn_body(q_ref, k_ref, v_ref, z_ref, ab_ref, cq_ref, ck_ref, cv_ref, alog_ref, dt_ref, onorm_ref,
              o_ref,
              qn_s, kn_s, vb_s, gc_s, beta_s, u_s, w_s, qd_s, kdta_s, kdtb_s, qk_s, cd_s, vnew_s, oacc_s):
    s_len = q_ref.shape[0]
    n_tiles = s_len // TILE
    h = pl.program_id(1)

    rows = lax.broadcasted_iota(jnp.int32, (s_len, HEAD_DIM), 0)
    lanes = lax.broadcasted_iota(jnp.int32, (s_len, HEAD_DIM), 1)

    def conv_silu(x_ref, cw_ref):
        x = x_ref[...].astype(F32)
        cw = cw_ref[...]
        acc = x * cw[GDN_CONV - 1:GDN_CONV, :]
        for i in range(GDN_CONV - 1):
            sh = GDN_CONV - 1 - i
            acc = acc + jnp.where(rows >= sh, pltpu.roll(x, sh, axis=0), 0.0) * cw[i:i + 1, :]
        return acc * _sigmoid(acc)

    def l2n(x):
        return x * lax.rsqrt(jnp.sum(x * x, axis=-1, keepdims=True) + EPS)

    ab = ab_ref[...]
    g_all = -jnp.exp(alog_ref[...]) * _softplus(ab + dt_ref[...])
    pos = rows % CHUNK
    sh = 1
    while sh < CHUNK:
        g_all = g_all + jnp.where(pos >= sh, pltpu.roll(g_all, sh, axis=0), 0.0)
        sh *= 2
    gc = jnp.sum(jnp.where(lanes == h, g_all, 0.0), axis=-1, keepdims=True)
    beta = _sigmoid(jnp.sum(jnp.where(lanes == h + HEADS, ab, 0.0), axis=-1, keepdims=True))
    egc = jnp.exp(gc)
    gc_s[...] = jnp.broadcast_to(gc, (s_len, HEAD_DIM))
    beta_s[...] = jnp.broadcast_to(beta, (s_len, HEAD_DIM))

    qn = l2n(conv_silu(q_ref, cq_ref)) * (HEAD_DIM ** -0.5)
    qn_s[...] = qn
    qd_s[...] = (qn * egc).astype(BF)
    kn_s[...] = l2n(conv_silu(k_ref, ck_ref))
    vb_s[...] = conv_silu(v_ref, cv_ref) * beta

    ri = lax.broadcasted_iota(jnp.int32, (TILE, TILE), 0)
    ci = lax.broadcasted_iota(jnp.int32, (TILE, TILE), 1)
    same_chunk = (ri // CHUNK) == (ci // CHUNK)
    incl = same_chunk & (ri >= ci)
    strict = same_chunk & (ri > ci)
    eye = (ri == ci).astype(F32)
    first_half_rows = ri < CHUNK
    first_half_cols = ci < CHUNK

    def tile_prep(t, carry):
        r0 = pl.multiple_of(t * TILE, TILE)
        sl = pl.ds(r0, TILE)
        k = kn_s[sl, :]
        q = qn_s[sl, :]
        gcb = gc_s[sl, :]
        bb = beta_s[sl, :]
        d = gcb - gcb.T
        decay = jnp.where(incl, jnp.exp(jnp.where(incl, d, 0.0)), 0.0)
        kb = k.astype(BF)
        kk = _dot_nt(kb, kb)
        m = jnp.where(strict, bb * kk * decay, 0.0)

        base = 16
        blk = (ri // base) == (ci // base)
        mp = jnp.where(blk, m, 0.0)
        p = eye - mp
        width = 2
        while width < base:
            mp = _dot_x3(mp, mp)
            p = p + _dot_x3(p, mp)
            width *= 2
        size = base
        while size < CHUNK:
            off = ((ri // (2 * size)) == (ci // (2 * size))) & ((ri // size) != (ci // size))
            c = jnp.where(off, m, 0.0)
            p = p - _dot_x3(_dot_x3(p, c), p)
            size *= 2

        vb = vb_s[sl, :]
        kbe = k * (bb * jnp.exp(gcb))
        u_s[sl, :] = _dot_x3(p, vb)
        w_s[sl, :] = _dot_x3(p, kbe).astype(BF)
        qk_s[t] = (_dot_nt(q.astype(BF), kb) * decay).astype(BF)

        gl_a = gcb[CHUNK - 1:CHUNK, :]
        gl_b = gcb[TILE - 1:TILE, :]
        gl = jnp.where(first_half_rows, gl_a, gl_b)
        kdt = (k * jnp.exp(gl - gcb)).T
        kdta_s[t] = jnp.where(first_half_cols, kdt, 0.0).astype(BF)
        kdtb_s[t] = jnp.where(first_half_cols, 0.0, kdt).astype(BF)
        cd_s[t, 0:1, :] = jnp.exp(gl_a)
        cd_s[t, 1:2, :] = jnp.exp(gl_b)
        return carry

    lax.fori_loop(0, n_tiles, tile_prep, 0)

    vnew_s[...] = jnp.zeros_like(vnew_s)

    def tile_scan(t, state):
        for half in range(TILE // CHUNK):
            rr = pl.multiple_of(t * TILE + half * CHUNK, CHUNK)
            sl = pl.ds(rr, CHUNK)
            hs = slice(half * CHUNK, (half + 1) * CHUNK)
            sb = state.astype(BF)
            v_new = u_s[sl, :] - _dot(w_s[sl, :], sb)
            vnew_s[hs, :] = v_new.astype(BF)
            vn = vnew_s[...]
            oacc_s[sl, :] = _dot(qd_s[sl, :], sb) + _dot(qk_s[t, hs, :], vn)
            kdt = kdta_s[t] if half == 0 else kdtb_s[t]
            state = state * cd_s[t, half:half + 1, :] + _dot(kdt, vn)
        return state

    lax.fori_loop(0, n_tiles, tile_scan, jnp.zeros((HEAD_DIM, HEAD_DIM), F32))

    z = z_ref[...].astype(F32)
    o_ref[...] = (_rms(oacc_s[...], onorm_ref[...]) * (z * _sigmoid(z))).astype(BF)


def _gdn(heads4, ab3, cw, alog_row, dt_row, onorm_row):
    _, b, s, _ = heads4.shape
    n_tiles = s // TILE

    def head_spec(base):
        return pl.BlockSpec((None, None, s, HEAD_DIM), lambda bi, hi: (base + hi, bi, 0, 0))

    def conv_spec(base):
        return pl.BlockSpec((GDN_CONV, HEAD_DIM), lambda bi, hi: (0, base + hi))

    row_spec = pl.BlockSpec((1, HEAD_DIM), lambda bi, hi: (0, 0))
    seq_f32 = pltpu.VMEM((s, HEAD_DIM), F32)
    seq_bf = pltpu.VMEM((s, HEAD_DIM), BF)
    tiles_bf = pltpu.VMEM((n_tiles, TILE, TILE), BF)
    return pl.pallas_call(
        _gdn_body,
        grid=(b, HEADS),
        in_specs=[head_spec(_GQ), head_spec(_GK), head_spec(_GV), head_spec(_GZ),
                  pl.BlockSpec((None, s, HEAD_DIM), lambda bi, hi: (bi, 0, 0)),
                  conv_spec(0), conv_spec(HEADS), conv_spec(2 * HEADS),
                  row_spec, row_spec, row_spec],
        out_specs=pl.BlockSpec((None, s, HEAD_DIM), lambda bi, hi: (bi, 0, hi)),
        out_shape=jax.ShapeDtypeStruct((b, s, HEADS * HEAD_DIM), BF),
        scratch_shapes=[seq_f32, seq_f32, seq_f32, seq_f32, seq_f32,
                        seq_f32, seq_bf, seq_bf,
                        tiles_bf, tiles_bf, tiles_bf,
                        pltpu.VMEM((n_tiles, 8, HEAD_DIM), F32),
                        pltpu.VMEM((TILE, HEAD_DIM), BF),
                        seq_f32],
        compiler_params=_params("parallel", "arbitrary"),
        name="gdn",
    )(heads4, heads4, heads4, heads4, ab3, cw, cw, cw, alog_row, dt_row, onorm_row)


def _sb_body(q_ref, k_ref, v_ref, o_ref, *, tq, tk):
    i = pl.program_id(2)
    q = (q_ref[...].astype(F32) * (HEAD_DIM ** -0.5)).astype(BF)
    t_idx = i * tq + lax.broadcasted_iota(jnp.int32, (tq, tk), 0)
    s_loc = lax.broadcasted_iota(jnp.int32, (tq, tk), 1)
    jr = lax.broadcasted_iota(jnp.int32, (tk, tk), 0)
    jc = lax.broadcasted_iota(jnp.int32, (tk, tk), 1)
    later = (jr > jc).astype(BF)
    n_kb = (i + 1) * (tq // tk)

    def body(jj, carry):
        acc, c = carry
        k0 = pl.multiple_of((n_kb - 1 - jj) * tk, tk)
        kj = k_ref[pl.ds(k0, tk), :]
        vj = v_ref[pl.ds(k0, tk), :]
        z = _dot_nt(q, kj)
        causal = (k0 + s_loc) < t_idx
        lf = jnp.where(causal, -_softplus(z), 0.0)
        hi, lo = _split(lf)
        suffix = _dot(hi, later) + _dot(lo, later)
        wgt = jnp.where(causal, jnp.exp(z + lf + suffix + c), 0.0)
        acc = acc + _dot(wgt.astype(BF), vj)
        c = c + jnp.sum(lf, axis=-1, keepdims=True)
        return acc, c

    acc, _ = lax.fori_loop(0, n_kb, body,
                           (jnp.zeros((tq, HEAD_DIM), F32), jnp.zeros((tq, 1), F32)))
    o_ref[...] = acc.astype(BF)


def _sb(heads4, tq, tk):
    _, b, s, _ = heads4.shape
    return pl.pallas_call(
        functools.partial(_sb_body, tq=tq, tk=tk),
        grid=(b, HEADS, s // tq),
        in_specs=[
            pl.BlockSpec((None, None, tq, HEAD_DIM), lambda bi, hi, i: (_SQ + hi, bi, i, 0)),
            pl.BlockSpec((None, None, s, HEAD_DIM), lambda bi, hi, i: (_SK + hi, bi, 0, 0)),
            pl.BlockSpec((None, None, s, HEAD_DIM), lambda bi, hi, i: (_SV + hi, bi, 0, 0)),
        ],
        out_specs=pl.BlockSpec((None, tq, HEAD_DIM), lambda bi, hi, i: (bi, i, hi)),
        out_shape=jax.ShapeDtypeStruct((b, s, HEADS * HEAD_DIM), BF),
        compiler_params=_params("parallel", "parallel", "arbitrary"),
        name="sb",
    )(heads4, heads4, heads4)


def _merge_body(oa_ref, ob_ref, xn_ref, x_ref, wg_ref, wpa_ref, wpb_ref, wo_ref, h_ref):
    gates = _dot(xn_ref[...], wg_ref[...])
    pa = _dot(oa_ref[...], wpa_ref[...])
    pb = _dot(ob_ref[...], wpb_ref[...])
    merged = _sigmoid(gates[:, :D_MODEL]) * pa + _sigmoid(gates[:, D_MODEL:]) * pb
    h_ref[...] = x_ref[...] + _dot(merged.astype(BF), wo_ref[...])


def _merge_out(oa, ob, xn, x2, w_gates, w_pa, w_pb, w_o, tm):
    t = x2.shape[0]
    tile = lambda: pl.BlockSpec((tm, D_MODEL), lambda i: (i, 0))
    return pl.pallas_call(
        _merge_body,
        grid=(t // tm,),
        in_specs=[tile(), tile(), tile(), tile(),
                  _const_spec((D_MODEL, 2 * D_MODEL)), _const_spec((D_MODEL, D_MODEL)),
                  _const_spec((D_MODEL, D_MODEL)), _const_spec((D_MODEL, D_MODEL))],
        out_specs=tile(),
        out_shape=jax.ShapeDtypeStruct((t, D_MODEL), F32),
        compiler_params=_params("parallel"),
        name="merge_out",
    )(oa, ob, xn, x2, w_gates, w_pa, w_pb, w_o)


def _mem_kv_body(mem_ref, g_ref, w_ref, kn_ref, k_ref, v_ref):
    mn = _rms(mem_ref[...], g_ref[...]).astype(BF)
    kv = _dot(mn, w_ref[...])
    for hh in range(X_HEADS):
        cs = slice(hh * X_DH, (hh + 1) * X_DH)
        k_ref[:, cs] = _rms(kv[:, cs], kn_ref[...]).astype(BF)
    v_ref[...] = kv[:, D_MODEL:].astype(BF)


def _mem_kv(mem, g, w_xkv, k_norm):
    b, m, _ = mem.shape
    blk = lambda: pl.BlockSpec((None, m, D_MODEL), lambda bi: (bi, 0, 0))
    return pl.pallas_call(
        _mem_kv_body,
        grid=(b,),
        in_specs=[blk(), _const_spec((1, D_MODEL)), _const_spec((D_MODEL, 2 * D_MODEL)),
                  _const_spec((1, X_DH))],
        out_specs=[blk(), blk()],
        out_shape=[jax.ShapeDtypeStruct((b, m, D_MODEL), BF)] * 2,
        compiler_params=_params("parallel"),
        name="mem_kv",
    )(mem, g, w_xkv, k_norm)


def _xattn_body(h_ref, k_ref, v_ref, g_ref, qn_ref, wq_ref, wo_ref, o_ref, att_s):
    h = h_ref[...]
    q = _dot(_rms(h, g_ref[...]).astype(BF), wq_ref[...])
    for hh in range(X_HEADS):
        cs = slice(hh * X_DH, (hh + 1) * X_DH)
        qh = _rms(q[:, cs], qn_ref[...]).astype(BF)
        sc = _dot_nt(qh, k_ref[:, cs]) * (X_DH ** -0.5)
        e = jnp.exp(sc - jnp.max(sc, axis=-1, keepdims=True))
        p = e / jnp.sum(e, axis=-1, keepdims=True)
        att_s[:, cs] = _dot(p.astype(BF), v_ref[:, cs]).astype(BF)
    o_ref[...] = h + _dot(att_s[...], wo_ref[...])


def _xattn(h3, kx, vx, g, q_norm, w_q, w_o, tm):
    b, s, _ = h3.shape
    m = kx.shape[1]
    tile = lambda: pl.BlockSpec((None, tm, D_MODEL), lambda bi, i: (bi, i, 0))
    kv = lambda: pl.BlockSpec((None, m, D_MODEL), lambda bi, i: (bi, 0, 0))
    return pl.pallas_call(
        _xattn_body,
        grid=(b, s // tm),
        in_specs=[tile(), kv(), kv(), _const_spec((1, D_MODEL)), _const_spec((1, X_DH)),
                  _const_spec((D_MODEL, D_MODEL)), _const_spec((D_MODEL, D_MODEL))],
        out_specs=tile(),
        out_shape=jax.ShapeDtypeStruct((b, s, D_MODEL), F32),
        scratch_shapes=[pltpu.VMEM((tm, D_MODEL), BF)],
        compiler_params=_params("parallel", "parallel"),
        name="xattn",
    )(h3, kx, vx, g, q_norm, w_q, w_o)


def _ffn_body(h_ref, hp_ref, g_ref, wu_ref, cw_ref, wd_ref, o_ref, hx_s, acc_s):
    tm = h_ref.shape[0]
    n_fc = wd_ref.shape[0]
    h = h_ref[...]
    g = g_ref[...]
    first = pl.program_id(1) == 0
    hx_s[0:FFN_HALO, :] = jnp.where(first, 0.0, _rms(hp_ref[...], g)).astype(BF)
    hx_s[FFN_HALO:, :] = _rms(h, g).astype(BF)
    acc_s[...] = h

    def conv(a, cw):
        y = a[FFN_HALO:, :] * cw[FFN_CONV - 1:FFN_CONV, :]
        for i in range(FFN_CONV - 1):
            sh = FFN_CONV - 1 - i
            y = y + pltpu.roll(a, sh, axis=0)[FFN_HALO:, :] * cw[i:i + 1, :]
        return y

    def body(f, carry):
        hx = hx_s[...]
        ua = conv(_dot(hx, wu_ref[0, f]), cw_ref[0, f])
        ug = conv(_dot(hx, wu_ref[1, f]), cw_ref[1, f])
        act = (ua * _sigmoid(ua) * ug).astype(BF)
        acc_s[...] += _dot(act, wd_ref[f])
        return carry

    lax.fori_loop(0, n_fc, body, 0)
    o_ref[...] = acc_s[...]


def _ffn(h3, g, wu4, cw4, wd3, tm):
    b, s, _ = h3.shape
    n_fc = wd3.shape[0]
    halo_blocks = tm // FFN_HALO
    return pl.pallas_call(
        _ffn_body,
        grid=(b, s // tm),
        in_specs=[
            pl.BlockSpec((None, tm, D_MODEL), lambda bi, i: (bi, i, 0)),
            pl.BlockSpec((None, FFN_HALO, D_MODEL),
                         lambda bi, i: (bi, jnp.maximum(i * halo_blocks - 1, 0), 0)),
            _const_spec((1, D_MODEL)),
            _const_spec((2, n_fc, D_MODEL, FFN_FC)),
            _const_spec((2, n_fc, FFN_CONV, FFN_FC)),
            _const_spec((n_fc, FFN_FC, D_MODEL)),
        ],
        out_specs=pl.BlockSpec((None, tm, D_MODEL), lambda bi, i: (bi, i, 0)),
        out_shape=jax.ShapeDtypeStruct((b, s, D_MODEL), F32),
        scratch_shapes=[pltpu.VMEM((tm + FFN_HALO, D_MODEL), BF), pltpu.VMEM((tm, D_MODEL), F32)],
        compiler_params=_params("parallel", "arbitrary"),
        name="ffn",
    )(h3, h3, g, wu4, cw4, wd3)


def _pick(n, pref):
    while n % pref:
        pref //= 2
    return pref


def _layer(h, mem, norm_mix, w_in, conv_gdn, a_log, dt_bias, gdn_out_norm, w_proj_gdn, w_proj_sb, w_out,
           norm_x, norm_mem, w_xq, w_xkv, xq_norm, xk_norm, w_xo, norm_ffn, w_up, conv_ffn, w_down):
    b, s, d = h.shape
    t = b * s
    assert d == D_MODEL and s % TILE == 0
    qkv_w = 3 * HEADS * HEAD_DIM
    ab_end = qkv_w + 2 * HEADS
    heads_end = ab_end + 4 * HEADS * HEAD_DIM

    w_heads = jnp.concatenate([w_in[:, :qkv_w], w_in[:, ab_end:heads_end]], axis=1).astype(BF)
    w_ab = jnp.pad(w_in[:, qkv_w:ab_end], ((0, 0), (0, HEAD_DIM - 2 * HEADS))).astype(BF)
    w_gates = w_in[:, heads_end:].astype(BF)
    pad_row = lambda v: jnp.pad(v, (0, HEAD_DIM - HEADS))[None, :]
    n_fc = D_FF // FFN_FC
    wu4 = w_up.astype(BF).reshape(D_MODEL, 2, n_fc, FFN_FC).transpose(1, 2, 0, 3)
    cw4 = conv_ffn.reshape(FFN_CONV, 2, n_fc, FFN_FC).transpose(1, 2, 0, 3)
    wd3 = w_down.astype(BF).reshape(n_fc, FFN_FC, D_MODEL)

    x2 = h.reshape(t, d)
    heads, xn, ab = _in_proj(x2, norm_mix[None, :], w_heads, w_ab, _pick(t, 1024))
    heads4 = heads.reshape(N_HEAD_BLOCKS, b, s, HEAD_DIM)
    o_a = _gdn(heads4, ab.reshape(b, s, HEAD_DIM), conv_gdn, pad_row(a_log), pad_row(dt_bias),
               gdn_out_norm[None, :])
    o_b = _sb(heads4, _pick(s, 256), TILE)
    h1 = _merge_out(o_a.reshape(t, d), o_b.reshape(t, d), xn, x2, w_gates,
                    w_proj_gdn.astype(BF), w_proj_sb.astype(BF), w_out.astype(BF), _pick(t, 512))
    kx, vx = _mem_kv(mem, norm_mem[None, :], w_xkv.astype(BF), xk_norm[None, :])
    h2 = _xattn(h1.reshape(b, s, d), kx, vx, norm_x[None, :], xq_norm[None, :],
                w_xq.astype(BF), w_xo.astype(BF), _pick(s, 512))
    return _ffn(h2, norm_ffn[None, :], wu4, cw4, wd3, _pick(s, 512))


def kernel(x, mem, norm_mix, w_in, conv_gdn, a_log, dt_bias, gdn_out_norm, w_proj_gdn, w_proj_sb, w_out,
           norm_x, norm_mem, w_xq, w_xkv, xq_norm, xk_norm, w_xo, norm_ffn, w_up, conv_ffn, w_down):
    h = x
    for l in range(norm_mix.shape[0]):
        h = _layer(h, mem, norm_mix[l], w_in[l], conv_gdn[l], a_log[l], dt_bias[l], gdn_out_norm[l],
                   w_proj_gdn[l], w_proj_sb[l], w_out[l], norm_x[l], norm_mem[l], w_xq[l], w_xkv[l],
                   xq_norm[l], xk_norm[l], w_xo[l], norm_ffn[l], w_up[l], conv_ffn[l], w_down[l])
    return h
```

```python
import functools

import jax
import jax.numpy as jnp
from jax import lax
from jax.experimental import pallas as pl
from jax.experimental.pallas import tpu as pltpu

F32 = jnp.float32
BF = jnp.bfloat16

D_MODEL = 1024
HEADS = 8
HEAD_DIM = 128
CHUNK = 64
TILE = 128
GDN_CONV = 4
SB_TK = 128
SB_KS = 512
X_HEADS = 4
X_DH = D_MODEL // X_HEADS
D_FF = 2816
FFN_CONV = 3
FFN_FC = 256
FFN_HALO = 16
EPS = 1e-6
VMEM_LIMIT = 56 * 1024 * 1024

_GQ, _GK, _GV, _GZ, _SQ, _SK, _SV = (i * HEADS for i in range(7))
N_HEAD_BLOCKS = 7 * HEADS


def _dot(a, b):
    return jnp.dot(a, b, preferred_element_type=F32)


def _dot_nt(a, b):
    return lax.dot_general(a, b, (((1,), (1,)), ((), ())), preferred_element_type=F32)


def _bdot(a, b):
    return _dot(a.astype(BF), b.astype(BF))


def _split(a):
    hi = a.astype(BF)
    lo = (a - hi.astype(F32)).astype(BF)
    return hi, lo


def _rms(x, g):
    return x * lax.rsqrt(jnp.mean(x * x, axis=-1, keepdims=True) + EPS) * g


def _softplus(x):
    return jnp.maximum(x, 0.0) + jnp.log1p(jnp.exp(-jnp.abs(x)))


def _sigmoid(x):
    return 1.0 / (1.0 + jnp.exp(-x))


def _params(*sem):
    return pltpu.CompilerParams(dimension_semantics=sem, vmem_limit_bytes=VMEM_LIMIT)


def _const_spec(shape):
    n = len(shape)
    return pl.BlockSpec(shape, lambda *_: (0,) * n, pipeline_mode=pl.Buffered(1))


def _in_proj_body(x_ref, g_ref, w_ref, wab_ref, heads_ref, xn_ref, ab_ref):
    @pl.when(pl.program_id(1) == 0)
    def _():
        xn = _rms(x_ref[...], g_ref[...]).astype(BF)
        xn_ref[...] = xn
        ab_ref[...] = _dot(xn, wab_ref[...])

    res = _dot(xn_ref[...], w_ref[...])
    for c in range(HEADS):
        heads_ref[c] = res[:, c * HEAD_DIM:(c + 1) * HEAD_DIM].astype(BF)


def _in_proj(x2, g, w_heads, w_ab, tm):
    t = x2.shape[0]
    nj = N_HEAD_BLOCKS // HEADS
    wn = HEADS * HEAD_DIM
    return pl.pallas_call(
        _in_proj_body,
        grid=(t // tm, nj),
        in_specs=[
            pl.BlockSpec((tm, D_MODEL), lambda i, j: (i, 0)),
            _const_spec((1, D_MODEL)),
            pl.BlockSpec((D_MODEL, wn), lambda i, j: (0, j)),
            _const_spec((D_MODEL, HEAD_DIM)),
        ],
        out_specs=[
            pl.BlockSpec((HEADS, tm, HEAD_DIM), lambda i, j: (j, i, 0)),
            pl.BlockSpec((tm, D_MODEL), lambda i, j: (i, 0)),
            pl.BlockSpec((tm, HEAD_DIM), lambda i, j: (i, 0)),
        ],
        out_shape=[
            jax.ShapeDtypeStruct((N_HEAD_BLOCKS, t, HEAD_DIM), BF),
            jax.ShapeDtypeStruct((t, D_MODEL), BF),
            jax.ShapeDtypeStruct((t, HEAD_DIM), F32),
        ],
        compiler_params=_params("parallel", "arbitrary"),
        name="in_proj",
    )(x2, g, w_heads, w_ab)


def _gdn_body(q_ref, k_ref, v_ref, z_ref, ab_ref, cq_ref, ck_ref, cv_ref, alog_ref, dt_ref, onorm_ref,
              o_ref,
              gcum_s, qb_s, qd_s, kn_s, vb_s, gc_s, beta_s, u_s, w_s, qk_s, a_s, b_s, cd_s):
    s_len = q_ref.shape[0]
    n_tiles = s_len // TILE
    halves = TILE // CHUNK
    h = pl.program_id(1)

    rows = lax.broadcasted_iota(jnp.int32, (s_len, HEAD_DIM), 0)
    lanes = lax.broadcasted_iota(jnp.int32, (s_len, HEAD_DIM), 1)

    def conv_silu(x_ref, cw_ref):
        x = x_ref[...].astype(F32)
        cw = cw_ref[...]
        acc = x * cw[GDN_CONV - 1:GDN_CONV, :]
        for i in range(GDN_CONV - 1):
            sh = GDN_CONV - 1 - i
            acc = acc + jnp.where(rows >= sh, pltpu.roll(x, sh, axis=0), 0.0) * cw[i:i + 1, :]
        return acc * _sigmoid(acc)

    def l2n(x):
        return x * lax.rsqrt(jnp.sum(x * x, axis=-1, keepdims=True) + EPS)

    @pl.when(h == 0)
    def _():
        g_all = -jnp.exp(alog_ref[...]) * _softplus(ab_ref[...] + dt_ref[...])
        pos = rows % CHUNK
        sh = 1
        while sh < CHUNK:
            g_all = g_all + jnp.where(pos >= sh, pltpu.roll(g_all, sh, axis=0), 0.0)
            sh *= 2
        gcum_s[...] = g_all

    gc = jnp.sum(jnp.where(lanes == h, gcum_s[...], 0.0), axis=-1, keepdims=True)
    beta = _sigmoid(jnp.sum(jnp.where(lanes == h + HEADS, ab_ref[...], 0.0), axis=-1, keepdims=True))
    gc_s[...] = jnp.broadcast_to(gc, (s_len, HEAD_DIM))
    beta_s[...] = jnp.broadcast_to(beta, (s_len, HEAD_DIM))

    qn = l2n(conv_silu(q_ref, cq_ref)) * (HEAD_DIM ** -0.5)
    qb_s[...] = qn.astype(BF)
    qd_s[...] = (qn * jnp.exp(gc)).astype(BF)
    kn_s[...] = l2n(conv_silu(k_ref, ck_ref))
    vb_s[...] = (conv_silu(v_ref, cv_ref) * beta).astype(BF)

    ri = lax.broadcasted_iota(jnp.int32, (TILE, TILE), 0)
    ci = lax.broadcasted_iota(jnp.int32, (TILE, TILE), 1)
    same_chunk = (ri // CHUNK) == (ci // CHUNK)
    incl = same_chunk & (ri >= ci)
    strict = same_chunk & (ri > ci)
    eye = (ri == ci).astype(F32)
    first_half_rows = ri < CHUNK
    first_half_cols = ci < CHUNK

    def prep_tiles(ts):
        each = lambda f, *xs: [f(*a) for a in zip(*xs)]
        sls = [pl.ds(pl.multiple_of(t * TILE, TILE), TILE) for t in ts]
        k = [kn_s[sl, :] for sl in sls]
        gcb = [gc_s[sl, :] for sl in sls]
        bb = [beta_s[sl, :] for sl in sls]
        vb = [vb_s[sl, :] for sl in sls]
        qb = [qb_s[sl, :] for sl in sls]

        decay = each(lambda g: jnp.where(incl, jnp.exp(jnp.where(incl, g - g.T, 0.0)), 0.0), gcb)
        kb = each(lambda x: x.astype(BF), k)
        kk = each(_dot_nt, kb, kb)
        m = each(lambda b_, kk_, d_: jnp.where(strict, b_ * kk_ * d_, 0.0), bb, kk, decay)

        base = 16
        base_blk = (ri // base) == (ci // base)
        mp = each(lambda m_: jnp.where(base_blk, m_, 0.0), m)
        p = each(lambda mp_: eye - mp_, mp)
        width = 2
        while width < base:
            mp = each(_bdot, mp, mp)
            p = each(lambda p_, mp_: p_ + _bdot(p_, mp_), p, mp)
            width *= 2
        size = base
        while size < CHUNK:
            off = ((ri // (2 * size)) == (ci // (2 * size))) & ((ri // size) != (ci // size))
            pc = each(lambda p_, m_: _bdot(p_, jnp.where(off, m_, 0.0)), p, m)
            p = each(lambda p_, pc_: p_ - _bdot(pc_, p_), p, pc)
            size *= 2

        rhs = each(lambda vb_, k_, b_, g_: jnp.concatenate([vb_, (k_ * (b_ * jnp.exp(g_))).astype(BF)], axis=1),
                   vb, k, bb, gcb)
        sol = each(lambda p_, r_: _dot(p_.astype(BF), r_), p, rhs)
        qk = each(lambda q_, kb_, d_: (_dot_nt(q_, kb_) * d_).astype(BF), qb, kb, decay)

        gl_a = [g[CHUNK - 1:CHUNK, :] for g in gcb]
        gl_b = [g[TILE - 1:TILE, :] for g in gcb]
        kdt = each(lambda k_, g_, a_, b_: (k_ * jnp.exp(jnp.where(first_half_rows, a_, b_) - g_)).T,
                   k, gcb, gl_a, gl_b)
        uw = each(lambda s_: jnp.concatenate([s_[:, :HEAD_DIM], -s_[:, HEAD_DIM:]], axis=1).astype(BF), sol)
        ba_a = each(lambda kd_, uw_: _dot(jnp.where(first_half_cols, kd_, 0.0).astype(BF), uw_), kdt, uw)
        ba_b = each(lambda kd_, uw_: _dot(jnp.where(first_half_cols, 0.0, kd_).astype(BF), uw_), kdt, uw)

        for i, t in enumerate(ts):
            u_s[sls[i], :] = sol[i][:, :HEAD_DIM]
            w_s[sls[i], :] = sol[i][:, HEAD_DIM:].astype(BF)
            qk_s[t] = qk[i]
            b_s[t, 0] = ba_a[i][:, :HEAD_DIM]
            b_s[t, 1] = ba_b[i][:, :HEAD_DIM]
            a_s[t, 0] = ba_a[i][:, HEAD_DIM:].astype(BF)
            a_s[t, 1] = ba_b[i][:, HEAD_DIM:].astype(BF)
            cd_s[t, 0:1, :] = jnp.exp(gl_a[i])
            cd_s[t, 1:2, :] = jnp.exp(gl_b[i])

    prep_unroll = 4 if n_tiles % 4 == 0 else (2 if n_tiles % 2 == 0 else 1)

    def prep_body(tt, carry):
        prep_tiles([tt * prep_unroll + j for j in range(prep_unroll)])
        return carry

    lax.fori_loop(0, n_tiles // prep_unroll, prep_body, 0)

    def scan_step(t_chain, state, t_out, sb_prev, t_store, o_prev):
        sb, o = [], None
        if t_chain is not None:
            a = [a_s[t_chain, half] for half in range(halves)]
            b = [b_s[t_chain, half] for half in range(halves)]
            cd = [cd_s[t_chain, half:half + 1, :] for half in range(halves)]
            sb.append(state.astype(BF))
            nxt = _dot(a[0], sb[0])
        if t_out is not None:
            sl = pl.ds(pl.multiple_of(t_out * TILE, TILE), TILE)
            u, w, qd = u_s[sl, :], w_s[sl, :], qd_s[sl, :]
            vn, oq = [], []
            for half in range(halves):
                hs = slice(half * CHUNK, (half + 1) * CHUNK)
                vn.append((u[hs, :] - _dot(w[hs, :], sb_prev[half])).astype(BF))
                oq.append(_dot(qd[hs, :], sb_prev[half]))
        if t_chain is not None:
            state = state * cd[0] + nxt + b[0]
            for half in range(1, halves):
                sb.append(state.astype(BF))
                nxt = _dot(a[half], sb[half])
                if half == halves - 1 and t_out is not None:
                    o = jnp.concatenate(oq, axis=0) + _dot(qk_s[t_out], jnp.concatenate(vn, axis=0))
                state = state * cd[half] + nxt + b[half]
        if t_out is not None and o is None:
            o = jnp.concatenate(oq, axis=0) + _dot(qk_s[t_out], jnp.concatenate(vn, axis=0))
        if t_store is not None:
            sl = pl.ds(pl.multiple_of(t_store * TILE, TILE), TILE)
            z = z_ref[sl, :].astype(F32)
            o_ref[sl, :] = (_rms(o_prev, onorm_ref[...]) * (z * _sigmoid(z))).astype(BF)
        return state, sb, o

    def scan_body(t, carry):
        state, sb, o = scan_step(t, carry[0], t - 1, carry[1:1 + halves], t - 2, carry[-1])
        return (state, *sb, o)

    state, sb, _ = scan_step(0, jnp.zeros((HEAD_DIM, HEAD_DIM), F32), None, None, None, None)
    if n_tiles == 1:
        _, _, o = scan_step(None, None, 0, sb, None, None)
    else:
        state, sb, o = scan_step(1, state, 0, sb, None, None)
        carry = lax.fori_loop(2, n_tiles, scan_body, (state, *sb, o))
        _, _, o = scan_step(None, None, n_tiles - 1, carry[1:1 + halves], n_tiles - 2, carry[-1])
    scan_step(None, None, None, None, n_tiles - 1, o)


def _gdn(heads4, ab3, cw, alog_row, dt_row, onorm_row):
    _, b, s, _ = heads4.shape
    n_tiles = s // TILE
    halves = TILE // CHUNK

    def head_spec(base):
        return pl.BlockSpec((None, None, s, HEAD_DIM), lambda bi, hi: (base + hi, bi, 0, 0))

    def conv_spec(base):
        return pl.BlockSpec((GDN_CONV, HEAD_DIM), lambda bi, hi: (0, base + hi))

    row_spec = pl.BlockSpec((1, HEAD_DIM), lambda bi, hi: (0, 0))
    seq_f32 = pltpu.VMEM((s, HEAD_DIM), F32)
    seq_bf = pltpu.VMEM((s, HEAD_DIM), BF)
    return pl.pallas_call(
        _gdn_body,
        grid=(b, HEADS),
        in_specs=[head_spec(_GQ), head_spec(_GK), head_spec(_GV), head_spec(_GZ),
                  pl.BlockSpec((None, s, HEAD_DIM), lambda bi, hi: (bi, 0, 0)),
                  conv_spec(0), conv_spec(HEADS), conv_spec(2 * HEADS),
                  row_spec, row_spec, row_spec],
        out_specs=pl.BlockSpec((None, s, HEAD_DIM), lambda bi, hi: (bi, 0, hi)),
        out_shape=jax.ShapeDtypeStruct((b, s, HEADS * HEAD_DIM), BF),
        scratch_shapes=[seq_f32,
                        seq_bf, seq_bf, seq_f32, seq_bf,
                        seq_f32, seq_f32,
                        seq_f32, seq_bf,
                        pltpu.VMEM((n_tiles, TILE, TILE), BF),
                        pltpu.VMEM((n_tiles, halves, HEAD_DIM, HEAD_DIM), BF),
                        pltpu.VMEM((n_tiles, halves, HEAD_DIM, HEAD_DIM), F32),
                        pltpu.VMEM((n_tiles, 8, HEAD_DIM), F32)],
        compiler_params=_params("parallel", "arbitrary"),
        name="gdn",
    )(heads4, heads4, heads4, heads4, ab3, cw, cw, cw, alog_row, dt_row, onorm_row)


def _sb_body(q_ref, k_ref, v_ref, r2_ref, o_ref, *, tq, ks):
    i = pl.program_id(2)
    n_sub = ks // SB_TK
    q = (q_ref[...].astype(F32) * (HEAD_DIM ** -0.5)).astype(BF)
    t_idx = i * tq + lax.broadcasted_iota(jnp.int32, (tq, ks), 0)
    s_loc = lax.broadcasted_iota(jnp.int32, (tq, ks), 1)
    n_sb = ((i + 1) * tq + ks - 1) // ks

    def step(jb, carry, masked):
        acc, c = carry
        k0 = pl.multiple_of(jb * ks, ks)
        z = _dot_nt(q, k_ref[pl.ds(k0, ks), :])
        lf = jnp.minimum(-z, 0.0) - jnp.log(1.0 + jnp.exp(-jnp.abs(z)))
        if masked:
            causal = (k0 + s_loc) < t_idx
            lf = jnp.where(causal, lf, 0.0)
        hi, lo = _split(lf)
        wparts = [None] * n_sub
        for m in reversed(range(n_sub)):
            cs = slice(m * SB_TK, (m + 1) * SB_TK)
            sr = _dot(jnp.concatenate([hi[:, cs], lo[:, cs]], axis=1), r2_ref[...])
            w = jnp.exp(z[:, cs] + lf[:, cs] + sr[:, :SB_TK] + c)
            if masked:
                w = jnp.where(causal[:, cs], w, 0.0)
            wparts[m] = w.astype(BF)
            c = c + sr[:, SB_TK:]
        acc = acc + _dot(jnp.concatenate(wparts, axis=1), v_ref[pl.ds(k0, ks), :])
        return acc, c

    init = (jnp.zeros((tq, HEAD_DIM), F32), jnp.zeros((tq, SB_TK), F32))
    carry = step(n_sb - 1, init, True)
    acc, _ = lax.fori_loop(0, n_sb - 1, lambda jj, cr: step(n_sb - 2 - jj, cr, False), carry)
    o_ref[...] = acc.astype(BF)


def _sb(heads4, tq, ks):
    _, b, s, _ = heads4.shape
    assert s % ks == 0 and (ks % tq == 0 or tq % ks == 0)
    jr = jnp.arange(2 * SB_TK)[:, None] % SB_TK
    jc = jnp.arange(2 * SB_TK)[None, :]
    r2 = jnp.where(jc < SB_TK, jr > jc, True).astype(BF)
    return pl.pallas_call(
        functools.partial(_sb_body, tq=tq, ks=ks),
        grid=(b, HEADS, s // tq),
        in_specs=[
            pl.BlockSpec((None, None, tq, HEAD_DIM), lambda bi, hi, i: (_SQ + hi, bi, i, 0)),
            pl.BlockSpec((None, None, s, HEAD_DIM), lambda bi, hi, i: (_SK + hi, bi, 0, 0)),
            pl.BlockSpec((None, None, s, HEAD_DIM), lambda bi, hi, i: (_SV + hi, bi, 0, 0)),
            _const_spec((2 * SB_TK, 2 * SB_TK)),
        ],
        out_specs=pl.BlockSpec((None, tq, HEAD_DIM), lambda bi, hi, i: (bi, i, hi)),
        out_shape=jax.ShapeDtypeStruct((b, s, HEADS * HEAD_DIM), BF),
        compiler_params=_params("parallel", "parallel", "arbitrary"),
        name="sb",
    )(heads4, heads4, heads4, r2)


def _merge_body(oa_ref, ob_ref, xn_ref, x_ref, wg_ref, wpa_ref, wpb_ref, wo_ref, h_ref):
    gates = _dot(xn_ref[...], wg_ref[...])
    pa = _dot(oa_ref[...], wpa_ref[...])
    pb = _dot(ob_ref[...], wpb_ref[...])
    merged = _sigmoid(gates[:, :D_MODEL]) * pa + _sigmoid(gates[:, D_MODEL:]) * pb
    h_ref[...] = x_ref[...] + _dot(merged.astype(BF), wo_ref[...])


def _merge_out(oa, ob, xn, x2, w_gates, w_pa, w_pb, w_o, tm):
    t = x2.shape[0]
    tile = lambda: pl.BlockSpec((tm, D_MODEL), lambda i: (i, 0))
    return pl.pallas_call(
        _merge_body,
        grid=(t // tm,),
        in_specs=[tile(), tile(), tile(), tile(),
                  _const_spec((D_MODEL, 2 * D_MODEL)), _const_spec((D_MODEL, D_MODEL)),
                  _const_spec((D_MODEL, D_MODEL)), _const_spec((D_MODEL, D_MODEL))],
        out_specs=tile(),
        out_shape=jax.ShapeDtypeStruct((t, D_MODEL), F32),
        compiler_params=_params("parallel"),
        name="merge_out",
    )(oa, ob, xn, x2, w_gates, w_pa, w_pb, w_o)


def _mem_kv_body(mem_ref, g_ref, w_ref, kn_ref, k_ref, v_ref):
    mn = _rms(mem_ref[...], g_ref[...]).astype(BF)
    kv = _dot(mn, w_ref[...])
    for hh in range(X_HEADS):
        cs = slice(hh * X_DH, (hh + 1) * X_DH)
        k_ref[:, cs] = _rms(kv[:, cs], kn_ref[...]).astype(BF)
    v_ref[...] = kv[:, D_MODEL:].astype(BF)


def _mem_kv(mem, g, w_xkv, k_norm):
    b, m, _ = mem.shape
    blk = lambda: pl.BlockSpec((None, m, D_MODEL), lambda bi: (bi, 0, 0))
    return pl.pallas_call(
        _mem_kv_body,
        grid=(b,),
        in_specs=[blk(), _const_spec((1, D_MODEL)), _const_spec((D_MODEL, 2 * D_MODEL)),
                  _const_spec((1, X_DH))],
        out_specs=[blk(), blk()],
        out_shape=[jax.ShapeDtypeStruct((b, m, D_MODEL), BF)] * 2,
        compiler_params=_params("parallel"),
        name="mem_kv",
    )(mem, g, w_xkv, k_norm)


def _xattn_body(h_ref, k_ref, v_ref, g_ref, qn_ref, wq_ref, wo_ref, o_ref, att_s):
    h = h_ref[...]
    q = _dot(_rms(h, g_ref[...]).astype(BF), wq_ref[...])
    for hh in range(X_HEADS):
        cs = slice(hh * X_DH, (hh + 1) * X_DH)
        qh = _rms(q[:, cs], qn_ref[...]).astype(BF)
        sc = _dot_nt(qh, k_ref[:, cs]) * (X_DH ** -0.5)
        e = jnp.exp(sc - jnp.max(sc, axis=-1, keepdims=True))
        p = e / jnp.sum(e, axis=-1, keepdims=True)
        att_s[:, cs] = _dot(p.astype(BF), v_ref[:, cs]).astype(BF)
    o_ref[...] = h + _dot(att_s[...], wo_ref[...])


def _xattn(h3, kx, vx, g, q_norm, w_q, w_o, tm):
    b, s, _ = h3.shape
    m = kx.shape[1]
    tile = lambda: pl.BlockSpec((None, tm, D_MODEL), lambda bi, i: (bi, i, 0))
    kv = lambda: pl.BlockSpec((None, m, D_MODEL), lambda bi, i: (bi, 0, 0))
    return pl.pallas_call(
        _xattn_body,
        grid=(b, s // tm),
        in_specs=[tile(), kv(), kv(), _const_spec((1, D_MODEL)), _const_spec((1, X_DH)),
                  _const_spec((D_MODEL, D_MODEL)), _const_spec((D_MODEL, D_MODEL))],
        out_specs=tile(),
        out_shape=jax.ShapeDtypeStruct((b, s, D_MODEL), F32),
        scratch_shapes=[pltpu.VMEM((tm, D_MODEL), BF)],
        compiler_params=_params("parallel", "parallel"),
        name="xattn",
    )(h3, kx, vx, g, q_norm, w_q, w_o)


def _ffn_body(h_ref, hp_ref, g_ref, wu_ref, cw_ref, wd_ref, o_ref, hx_s, acc_s):
    n_fc = wd_ref.shape[0]
    h = h_ref[...]
    g = g_ref[...]
    first = pl.program_id(1) == 0
    hx_s[0:FFN_HALO, :] = jnp.where(first, 0.0, _rms(hp_ref[...], g)).astype(BF)
    hx_s[FFN_HALO:, :] = _rms(h, g).astype(BF)
    acc_s[...] = h

    def conv(a, cw):
        y = a[FFN_HALO:, :] * cw[FFN_CONV - 1:FFN_CONV, :]
        for i in range(FFN_CONV - 1):
            sh = FFN_CONV - 1 - i
            y = y + pltpu.roll(a, sh, axis=0)[FFN_HALO:, :] * cw[i:i + 1, :]
        return y

    def body(f, carry):
        hx = hx_s[...]
        ua = conv(_dot(hx, wu_ref[0, f]), cw_ref[0, f])
        ug = conv(_dot(hx, wu_ref[1, f]), cw_ref[1, f])
        act = (ua * _sigmoid(ua) * ug).astype(BF)
        acc_s[...] += _dot(act, wd_ref[f])
        return carry

    lax.fori_loop(0, n_fc, body, 0)
    o_ref[...] = acc_s[...]


def _ffn(h3, g, wu4, cw4, wd3, tm):
    b, s, _ = h3.shape
    n_fc = wd3.shape[0]
    halo_blocks = tm // FFN_HALO
    return pl.pallas_call(
        _ffn_body,
        grid=(b, s // tm),
        in_specs=[
            pl.BlockSpec((None, tm, D_MODEL), lambda bi, i: (bi, i, 0)),
            pl.BlockSpec((None, FFN_HALO, D_MODEL),
                         lambda bi, i: (bi, jnp.maximum(i * halo_blocks - 1, 0), 0)),
            _const_spec((1, D_MODEL)),
            _const_spec((2, n_fc, D_MODEL, FFN_FC)),
            _const_spec((2, n_fc, FFN_CONV, FFN_FC)),
            _const_spec((n_fc, FFN_FC, D_MODEL)),
        ],
        out_specs=pl.BlockSpec((None, tm, D_MODEL), lambda bi, i: (bi, i, 0)),
        out_shape=jax.ShapeDtypeStruct((b, s, D_MODEL), F32),
        scratch_shapes=[pltpu.VMEM((tm + FFN_HALO, D_MODEL), BF), pltpu.VMEM((tm, D_MODEL), F32)],
        compiler_params=_params("parallel", "arbitrary"),
        name="ffn",
    )(h3, h3, g, wu4, cw4, wd3)


def _pick(n, pref):
    while n % pref:
        pref //= 2
    return pref


def _layer(h, mem, norm_mix, w_in, conv_gdn, a_log, dt_bias, gdn_out_norm, w_proj_gdn, w_proj_sb, w_out,
           norm_x, norm_mem, w_xq, w_xkv, xq_norm, xk_norm, w_xo, norm_ffn, w_up, conv_ffn, w_down):
    b, s, d = h.shape
    t = b * s
    assert d == D_MODEL and s % TILE == 0
    qkv_w = 3 * HEADS * HEAD_DIM
    ab_end = qkv_w + 2 * HEADS
    heads_end = ab_end + 4 * HEADS * HEAD_DIM

    w_heads = jnp.concatenate([w_in[:, :qkv_w], w_in[:, ab_end:heads_end]], axis=1).astype(BF)
    w_ab = jnp.pad(w_in[:, qkv_w:ab_end], ((0, 0), (0, HEAD_DIM - 2 * HEADS))).astype(BF)
    w_gates = w_in[:, heads_end:].astype(BF)
    pad_row = lambda v: jnp.pad(v, (0, HEAD_DIM - HEADS))[None, :]
    n_fc = D_FF // FFN_FC
    wu4 = w_up.astype(BF).reshape(D_MODEL, 2, n_fc, FFN_FC).transpose(1, 2, 0, 3)
    cw4 = conv_ffn.reshape(FFN_CONV, 2, n_fc, FFN_FC).transpose(1, 2, 0, 3)
    wd3 = w_down.astype(BF).reshape(n_fc, FFN_FC, D_MODEL)

    x2 = h.reshape(t, d)
    heads, xn, ab = _in_proj(x2, norm_mix[None, :], w_heads, w_ab, _pick(t, 1024))
    heads4 = heads.reshape(N_HEAD_BLOCKS, b, s, HEAD_DIM)
    o_a = _gdn(heads4, ab.reshape(b, s, HEAD_DIM), conv_gdn, pad_row(a_log), pad_row(dt_bias),
               gdn_out_norm[None, :])
    sb_ks = _pick(s, SB_KS)
    o_b = _sb(heads4, sb_ks, sb_ks)
    h1 = _merge_out(o_a.reshape(t, d), o_b.reshape(t, d), xn, x2, w_gates,
                    w_proj_gdn.astype(BF), w_proj_sb.astype(BF), w_out.astype(BF), _pick(t, 512))
    kx, vx = _mem_kv(mem, norm_mem[None, :], w_xkv.astype(BF), xk_norm[None, :])
    h2 = _xattn(h1.reshape(b, s, d), kx, vx, norm_x[None, :], xq_norm[None, :],
                w_xq.astype(BF), w_xo.astype(BF), _pick(s, 512))
    return _ffn(h2, norm_ffn[None, :], wu4, cw4, wd3, _pick(s, 512))


def kernel(x, mem, norm_mix, w_in, conv_gdn, a_log, dt_bias, gdn_out_norm, w_proj_gdn, w_proj_sb, w_out,
           norm_x, norm_mem, w_xq, w_xkv, xq_norm, xk_norm, w_xo, norm_ffn, w_up, conv_ffn, w_down):
    h = x
    for l in range(norm_mix.shape[0]):
        h = _layer(h, mem, norm_mix[l], w_in[l], conv_gdn[l], a_log[l], dt_bias[l], gdn_out_norm[l],
                   w_proj_gdn[l], w_proj_sb[l], w_out[l], norm_x[l], norm_mem[l], w_xq[l], w_xkv[l],
                   xq_norm[l], xk_norm[l], w_xo[l], norm_ffn[l], w_up[l], conv_ffn[l], w_down[l])
    return h
```

```python
import functools

import jax
import jax.numpy as jnp
from jax import lax
from jax.experimental import pallas as pl
from jax.experimental.pallas import tpu as pltpu

F32 = jnp.float32
BF = jnp.bfloat16

D_MODEL = 1024
HEADS = 8
HEAD_DIM = 128
CHUNK = 64
TILE = 128
GDN_CONV = 4
GDN_PREP_TILES = 8
SB_TK = 128
SB_BLK = 512
X_HEADS = 4
X_DH = D_MODEL // X_HEADS
D_FF = 2816
FFN_CONV = 3
FFN_FC = 256
FFN_HALO = 16
EPS = 1e-6
LOG2E = 1.4426950408889634
VMEM_LIMIT = 56 * 1024 * 1024

_GQ, _GK, _GV, _GZ, _SQ, _SK, _SV = (i * HEADS for i in range(7))
N_HEAD_BLOCKS = 7 * HEADS


def _dot(a, b):
    return jnp.dot(a, b, preferred_element_type=F32)


def _dot_nt(a, b):
    return lax.dot_general(a, b, (((1,), (1,)), ((), ())), preferred_element_type=F32)


def _bdot(a, b):
    return _dot(a.astype(BF), b.astype(BF))


def _split(a):
    hi = a.astype(BF)
    lo = (a - hi.astype(F32)).astype(BF)
    return hi, lo


def _rms(x, g):
    return x * lax.rsqrt(jnp.mean(x * x, axis=-1, keepdims=True) + EPS) * g


def _softplus(x):
    return jnp.maximum(x, 0.0) + jnp.log1p(jnp.exp(-jnp.abs(x)))


def _sigmoid(x):
    return 1.0 / (1.0 + jnp.exp(-x))


def _params(*sem):
    return pltpu.CompilerParams(dimension_semantics=sem, vmem_limit_bytes=VMEM_LIMIT)


def _const_spec(shape):
    n = len(shape)
    return pl.BlockSpec(shape, lambda *_: (0,) * n, pipeline_mode=pl.Buffered(1))


def _in_proj_body(x_ref, g_ref, w_ref, wab_ref, heads_ref, xn_ref, ab_ref):
    @pl.when(pl.program_id(1) == 0)
    def _():
        xn = _rms(x_ref[...], g_ref[...]).astype(BF)
        xn_ref[...] = xn
        ab_ref[...] = _dot(xn, wab_ref[...])

    res = _dot(xn_ref[...], w_ref[...])
    for c in range(HEADS):
        heads_ref[c] = res[:, c * HEAD_DIM:(c + 1) * HEAD_DIM].astype(BF)


def _in_proj(x2, g, w_heads, w_ab, tm):
    t = x2.shape[0]
    nj = N_HEAD_BLOCKS // HEADS
    wn = HEADS * HEAD_DIM
    return pl.pallas_call(
        _in_proj_body,
        grid=(t // tm, nj),
        in_specs=[
            pl.BlockSpec((tm, D_MODEL), lambda i, j: (i, 0)),
            _const_spec((1, D_MODEL)),
            pl.BlockSpec((D_MODEL, wn), lambda i, j: (0, j)),
            _const_spec((D_MODEL, HEAD_DIM)),
        ],
        out_specs=[
            pl.BlockSpec((HEADS, tm, HEAD_DIM), lambda i, j: (j, i, 0)),
            pl.BlockSpec((tm, D_MODEL), lambda i, j: (i, 0)),
            pl.BlockSpec((tm, HEAD_DIM), lambda i, j: (i, 0)),
        ],
        out_shape=[
            jax.ShapeDtypeStruct((N_HEAD_BLOCKS, t, HEAD_DIM), BF),
            jax.ShapeDtypeStruct((t, D_MODEL), BF),
            jax.ShapeDtypeStruct((t, HEAD_DIM), F32),
        ],
        compiler_params=_params("parallel", "arbitrary"),
        name="in_proj",
    )(x2, g, w_heads, w_ab)


def _gdn_body(q_ref, k_ref, v_ref, z_ref, ab_ref, cq_ref, ck_ref, cv_ref, alog_ref, dt_ref, onorm_ref,
              o_ref,
              gsplit_s, qb_s, qd_s, kn_s, vb_s, gc_s, beta_s, u_s, w_s, qk_s, a_s, b_s, cd_s):
    s_len = q_ref.shape[0]
    n_tiles = s_len // TILE
    halves = TILE // CHUNK
    h = pl.program_id(1)

    def conv_silu(x_ref, cw_ref):
        x = x_ref[...].astype(F32)
        cw = cw_ref[...]
        xz = jnp.concatenate([jnp.zeros((8, HEAD_DIM), F32), x], axis=0)
        acc = x * cw[GDN_CONV - 1:GDN_CONV, :]
        for i in range(GDN_CONV - 1):
            acc = acc + pltpu.roll(xz, GDN_CONV - 1 - i, axis=0)[8:, :] * cw[i:i + 1, :]
        return acc * _sigmoid(acc)

    def l2n(x):
        return x * lax.rsqrt(jnp.sum(x * x, axis=-1, keepdims=True) + EPS)

    @pl.when(h == 0)
    def _():
        rows = lax.broadcasted_iota(jnp.int32, (s_len, HEAD_DIM), 0)
        lanes = lax.broadcasted_iota(jnp.int32, (s_len, HEAD_DIM), 1)
        ab = ab_ref[...]
        g_all = -jnp.exp(alog_ref[...]) * _softplus(ab + dt_ref[...])
        pos = rows % CHUNK
        sh = 1
        while sh < CHUNK:
            g_all = g_all + jnp.where(pos >= sh, pltpu.roll(g_all, sh, axis=0), 0.0)
            sh *= 2
        x = jnp.where(lanes < HEADS, g_all, ab)
        for piece in range(3):
            xb = x.astype(BF)
            gsplit_s[:, piece * HEAD_DIM:(piece + 1) * HEAD_DIM] = xb
            x = x - xb.astype(F32)

    sel_r = lax.broadcasted_iota(jnp.int32, (3 * HEAD_DIM, 2 * HEAD_DIM), 0) % HEAD_DIM
    sel_c = lax.broadcasted_iota(jnp.int32, (3 * HEAD_DIM, 2 * HEAD_DIM), 1)
    onehot = (sel_r == jnp.where(sel_c < HEAD_DIM, h, h + HEADS)).astype(BF)
    sel = _dot(gsplit_s[...], onehot)
    gcb = sel[:, :HEAD_DIM]
    beta = _sigmoid(sel[:, HEAD_DIM:])
    gc_s[...] = gcb
    beta_s[...] = beta

    qn = l2n(conv_silu(q_ref, cq_ref)) * (HEAD_DIM ** -0.5)
    qb_s[...] = qn.astype(BF)
    qd_s[...] = (qn * jnp.exp(gcb)).astype(BF)
    kn_s[...] = l2n(conv_silu(k_ref, ck_ref))
    vb_s[...] = (conv_silu(v_ref, cv_ref) * beta).astype(BF)

    ri = lax.broadcasted_iota(jnp.int32, (TILE, TILE), 0)
    ci = lax.broadcasted_iota(jnp.int32, (TILE, TILE), 1)
    same_chunk = (ri // CHUNK) == (ci // CHUNK)
    incl = same_chunk & (ri >= ci)
    strict = same_chunk & (ri > ci)
    eye = (ri == ci).astype(F32)
    first_half_rows = ri < CHUNK
    first_half_cols = ci < CHUNK

    def prep_tiles(ts):
        each = lambda f, *xs: [f(*a) for a in zip(*xs)]
        sls = [pl.ds(pl.multiple_of(t * TILE, TILE), TILE) for t in ts]
        k = [kn_s[sl, :] for sl in sls]
        gcb = [gc_s[sl, :] for sl in sls]
        bb = [beta_s[sl, :] for sl in sls]
        vb = [vb_s[sl, :] for sl in sls]
        qb = [qb_s[sl, :] for sl in sls]

        decay = each(lambda g: jnp.where(incl, jnp.exp(jnp.where(incl, g - g.T, 0.0)), 0.0), gcb)
        kb = each(lambda x: x.astype(BF), k)
        kk = each(_dot_nt, kb, kb)
        m = each(lambda b_, kk_, d_: jnp.where(strict, b_ * kk_ * d_, 0.0), bb, kk, decay)

        base = 16
        base_blk = (ri // base) == (ci // base)
        mp = each(lambda m_: jnp.where(base_blk, m_, 0.0), m)
        p = each(lambda mp_: eye - mp_, mp)
        width = 2
        while width < base:
            mp = each(_bdot, mp, mp)
            p = each(lambda p_, mp_: p_ + _bdot(p_, mp_), p, mp)
            width *= 2
        size = base
        while size < CHUNK:
            off = ((ri // (2 * size)) == (ci // (2 * size))) & ((ri // size) != (ci // size))
            pc = each(lambda p_, m_: _bdot(p_, jnp.where(off, m_, 0.0)), p, m)
            p = each(lambda p_, pc_: p_ - _bdot(pc_, p_), p, pc)
            size *= 2

        rhs = each(lambda vb_, k_, b_, g_: jnp.concatenate([vb_, (k_ * (b_ * jnp.exp(g_))).astype(BF)], axis=1),
                   vb, k, bb, gcb)
        sol = each(lambda p_, r_: _dot(p_.astype(BF), r_), p, rhs)
        qk = each(lambda q_, kb_, d_: (_dot_nt(q_, kb_) * d_).astype(BF), qb, kb, decay)

        gl_a = [g[CHUNK - 1:CHUNK, :] for g in gcb]
        gl_b = [g[TILE - 1:TILE, :] for g in gcb]
        kdt = each(lambda k_, g_, a_, b_: (k_ * jnp.exp(jnp.where(first_half_rows, a_, b_) - g_)).T,
                   k, gcb, gl_a, gl_b)
        uw = each(lambda s_: jnp.concatenate([s_[:, :HEAD_DIM], -s_[:, HEAD_DIM:]], axis=1).astype(BF), sol)
        ba_a = each(lambda kd_, uw_: _dot(jnp.where(first_half_cols, kd_, 0.0).astype(BF), uw_), kdt, uw)
        ba_b = each(lambda kd_, uw_: _dot(jnp.where(first_half_cols, 0.0, kd_).astype(BF), uw_), kdt, uw)

        for i, t in enumerate(ts):
            u_s[sls[i], :] = sol[i][:, :HEAD_DIM]
            w_s[sls[i], :] = sol[i][:, HEAD_DIM:].astype(BF)
            qk_s[t] = qk[i]
            b_s[t, 0] = ba_a[i][:, :HEAD_DIM]
            b_s[t, 1] = ba_b[i][:, :HEAD_DIM]
            a_s[t, 0] = ba_a[i][:, HEAD_DIM:].astype(BF)
            a_s[t, 1] = ba_b[i][:, HEAD_DIM:].astype(BF)
            cd_s[t, 0:1, :] = jnp.exp(gl_a[i])
            cd_s[t, 1:2, :] = jnp.exp(gl_b[i])

    prep_unroll = _pick(n_tiles, GDN_PREP_TILES)

    def prep_body(tt, carry):
        prep_tiles([tt * prep_unroll + j for j in range(prep_unroll)])
        return carry

    lax.fori_loop(0, n_tiles // prep_unroll, prep_body, 0)

    def scan_step(t_chain, state, t_out, sb_prev, t_store, o_prev):
        sb, o = [], None
        if t_chain is not None:
            a = [a_s[t_chain, half] for half in range(halves)]
            b = [b_s[t_chain, half] for half in range(halves)]
            cd = [cd_s[t_chain, half:half + 1, :] for half in range(halves)]
            sb.append(state.astype(BF))
            nxt = _dot(a[0], sb[0])
        if t_out is not None:
            sl = pl.ds(pl.multiple_of(t_out * TILE, TILE), TILE)
            u, w, qd = u_s[sl, :], w_s[sl, :], qd_s[sl, :]
            vn, oq = [], []
            for half in range(halves):
                hs = slice(half * CHUNK, (half + 1) * CHUNK)
                vn.append((u[hs, :] - _dot(w[hs, :], sb_prev[half])).astype(BF))
                oq.append(_dot(qd[hs, :], sb_prev[half]))
        if t_chain is not None:
            state = state * cd[0] + nxt + b[0]
            for half in range(1, halves):
                sb.append(state.astype(BF))
                nxt = _dot(a[half], sb[half])
                if half == halves - 1 and t_out is not None:
                    o = jnp.concatenate(oq, axis=0) + _dot(qk_s[t_out], jnp.concatenate(vn, axis=0))
                state = state * cd[half] + nxt + b[half]
        if t_out is not None and o is None:
            o = jnp.concatenate(oq, axis=0) + _dot(qk_s[t_out], jnp.concatenate(vn, axis=0))
        if t_store is not None:
            sl = pl.ds(pl.multiple_of(t_store * TILE, TILE), TILE)
            z = z_ref[sl, :].astype(F32)
            o_ref[sl, :] = (_rms(o_prev, onorm_ref[...]) * (z * _sigmoid(z))).astype(BF)
        return state, sb, o

    def scan_body(t, carry):
        state, sb, o = scan_step(t, carry[0], t - 1, carry[1:1 + halves], t - 2, carry[-1])
        return (state, *sb, o)

    state, sb, _ = scan_step(0, jnp.zeros((HEAD_DIM, HEAD_DIM), F32), None, None, None, None)
    if n_tiles == 1:
        _, _, o = scan_step(None, None, 0, sb, None, None)
    else:
        state, sb, o = scan_step(1, state, 0, sb, None, None)
        carry = lax.fori_loop(2, n_tiles, scan_body, (state, *sb, o))
        _, _, o = scan_step(None, None, n_tiles - 1, carry[1:1 + halves], n_tiles - 2, carry[-1])
    scan_step(None, None, None, None, n_tiles - 1, o)


def _gdn(heads4, ab3, cw, alog_row, dt_row, onorm_row):
    _, b, s, _ = heads4.shape
    n_tiles = s // TILE
    halves = TILE // CHUNK

    def head_spec(base):
        return pl.BlockSpec((None, None, s, HEAD_DIM), lambda bi, hi: (base + hi, bi, 0, 0))

    def conv_spec(base):
        return pl.BlockSpec((GDN_CONV, HEAD_DIM), lambda bi, hi: (0, base + hi))

    row_spec = pl.BlockSpec((1, HEAD_DIM), lambda bi, hi: (0, 0))
    seq_f32 = pltpu.VMEM((s, HEAD_DIM), F32)
    seq_bf = pltpu.VMEM((s, HEAD_DIM), BF)
    return pl.pallas_call(
        _gdn_body,
        grid=(b, HEADS),
        in_specs=[head_spec(_GQ), head_spec(_GK), head_spec(_GV), head_spec(_GZ),
                  pl.BlockSpec((None, s, HEAD_DIM), lambda bi, hi: (bi, 0, 0)),
                  conv_spec(0), conv_spec(HEADS), conv_spec(2 * HEADS),
                  row_spec, row_spec, row_spec],
        out_specs=pl.BlockSpec((None, s, HEAD_DIM), lambda bi, hi: (bi, 0, hi)),
        out_shape=jax.ShapeDtypeStruct((b, s, HEADS * HEAD_DIM), BF),
        scratch_shapes=[pltpu.VMEM((s, 3 * HEAD_DIM), BF),
                        seq_bf, seq_bf, seq_f32, seq_bf,
                        seq_f32, seq_f32,
                        seq_f32, seq_bf,
                        pltpu.VMEM((n_tiles, TILE, TILE), BF),
                        pltpu.VMEM((n_tiles, halves, HEAD_DIM, HEAD_DIM), BF),
                        pltpu.VMEM((n_tiles, halves, HEAD_DIM, HEAD_DIM), F32),
                        pltpu.VMEM((n_tiles, 8, HEAD_DIM), F32)],
        compiler_params=_params("parallel", "arbitrary"),
        name="gdn",
    )(heads4, heads4, heads4, heads4, ab3, cw, cw, cw, alog_row, dt_row, onorm_row)


def _neg_abs(x):
    sign = jnp.uint32(0x80000000)
    return lax.bitcast_convert_type(lax.bitcast_convert_type(x, jnp.uint32) | sign, F32)


def _sb_body(q_ref, k_ref, v_ref, r2_ref, o_ref, *, blk):
    s_len = q_ref.shape[0]
    n_sub = blk // SB_TK
    ri = lax.broadcasted_iota(jnp.int32, (blk, blk), 0)
    ci = lax.broadcasted_iota(jnp.int32, (blk, blk), 1)
    causal = ci < ri

    def step(q, jb, carry, masked):
        acc, c = carry
        ks_ = slice(jb * blk, (jb + 1) * blk)
        z2 = _dot_nt(q, k_ref[ks_, :])
        sp = jnp.maximum(z2, 0.0) + jnp.log2(1.0 + jnp.exp2(_neg_abs(z2)))
        if masked:
            sp = jnp.where(causal, sp, 0.0)
        hi, lo = _split(sp)
        wparts = [None] * n_sub
        for m in reversed(range(n_sub)):
            cs = slice(m * SB_TK, (m + 1) * SB_TK)
            sr = _dot(jnp.concatenate([hi[:, cs], lo[:, cs]], axis=1), r2_ref[...])
            w = jnp.exp2((z2[:, cs] - sp[:, cs]) + sr[:, :SB_TK] + c)
            if masked:
                w = jnp.where(causal[:, cs], w, 0.0)
            wparts[m] = w.astype(BF)
            c = c + sr[:, SB_TK:]
        return acc + _dot(jnp.concatenate(wparts, axis=1), v_ref[ks_, :]), c

    for qi in range(s_len // blk):
        qs = slice(qi * blk, (qi + 1) * blk)
        q = (q_ref[qs, :].astype(F32) * (HEAD_DIM ** -0.5 * LOG2E)).astype(BF)
        carry = (jnp.zeros((blk, HEAD_DIM), F32), jnp.zeros((blk, SB_TK), F32))
        for jb in range(qi, -1, -1):
            carry = step(q, jb, carry, jb == qi)
        o_ref[qs, :] = carry[0].astype(BF)


def _sb(heads4, blk):
    _, b, s, _ = heads4.shape
    assert s % blk == 0 and blk % SB_TK == 0
    jr = jnp.arange(2 * SB_TK)[:, None] % SB_TK
    jc = jnp.arange(2 * SB_TK)[None, :]
    r2 = -jnp.where(jc < SB_TK, jr > jc, True).astype(BF)
    seq = lambda base: pl.BlockSpec((None, None, s, HEAD_DIM), lambda bi, hi: (base + hi, bi, 0, 0))
    return pl.pallas_call(
        functools.partial(_sb_body, blk=blk),
        grid=(b, HEADS),
        in_specs=[seq(_SQ), seq(_SK), seq(_SV), _const_spec((2 * SB_TK, 2 * SB_TK))],
        out_specs=pl.BlockSpec((None, s, HEAD_DIM), lambda bi, hi: (bi, 0, hi)),
        out_shape=jax.ShapeDtypeStruct((b, s, HEADS * HEAD_DIM), BF),
        compiler_params=_params("parallel", "parallel"),
        name="sb",
    )(heads4, heads4, heads4, r2)


def _merge_body(oa_ref, ob_ref, xn_ref, x_ref, wg_ref, wpa_ref, wpb_ref, wo_ref, h_ref):
    gates = _dot(xn_ref[...], wg_ref[...])
    pa = _dot(oa_ref[...], wpa_ref[...])
    pb = _dot(ob_ref[...], wpb_ref[...])
    merged = _sigmoid(gates[:, :D_MODEL]) * pa + _sigmoid(gates[:, D_MODEL:]) * pb
    h_ref[...] = x_ref[...] + _dot(merged.astype(BF), wo_ref[...])


def _merge_out(oa, ob, xn, x2, w_gates, w_pa, w_pb, w_o, tm):
    t = x2.shape[0]
    tile = lambda: pl.BlockSpec((tm, D_MODEL), lambda i: (i, 0))
    return pl.pallas_call(
        _merge_body,
        grid=(t // tm,),
        in_specs=[tile(), tile(), tile(), tile(),
                  _const_spec((D_MODEL, 2 * D_MODEL)), _const_spec((D_MODEL, D_MODEL)),
                  _const_spec((D_MODEL, D_MODEL)), _const_spec((D_MODEL, D_MODEL))],
        out_specs=tile(),
        out_shape=jax.ShapeDtypeStruct((t, D_MODEL), F32),
        compiler_params=_params("parallel"),
        name="merge_out",
    )(oa, ob, xn, x2, w_gates, w_pa, w_pb, w_o)


def _mem_kv_body(mem_ref, g_ref, w_ref, kn_ref, k_ref, v_ref):
    mn = _rms(mem_ref[...], g_ref[...]).astype(BF)
    kv = _dot(mn, w_ref[...])
    for hh in range(X_HEADS):
        cs = slice(hh * X_DH, (hh + 1) * X_DH)
        k_ref[:, cs] = _rms(kv[:, cs], kn_ref[...]).astype(BF)
    v_ref[...] = kv[:, D_MODEL:].astype(BF)


def _mem_kv(mem, g, w_xkv, k_norm):
    b, m, _ = mem.shape
    blk = lambda: pl.BlockSpec((None, m, D_MODEL), lambda bi: (bi, 0, 0))
    return pl.pallas_call(
        _mem_kv_body,
        grid=(b,),
        in_specs=[blk(), _const_spec((1, D_MODEL)), _const_spec((D_MODEL, 2 * D_MODEL)),
                  _const_spec((1, X_DH))],
        out_specs=[blk(), blk()],
        out_shape=[jax.ShapeDtypeStruct((b, m, D_MODEL), BF)] * 2,
        compiler_params=_params("parallel"),
        name="mem_kv",
    )(mem, g, w_xkv, k_norm)


def _xattn_body(h_ref, k_ref, v_ref, g_ref, qn_ref, wq_ref, wo_ref, o_ref, att_s):
    h = h_ref[...]
    q = _dot(_rms(h, g_ref[...]).astype(BF), wq_ref[...])
    for hh in range(X_HEADS):
        cs = slice(hh * X_DH, (hh + 1) * X_DH)
        qh = _rms(q[:, cs], qn_ref[...]).astype(BF)
        sc = _dot_nt(qh, k_ref[:, cs]) * (X_DH ** -0.5)
        e = jnp.exp(sc - jnp.max(sc, axis=-1, keepdims=True))
        p = e / jnp.sum(e, axis=-1, keepdims=True)
        att_s[:, cs] = _dot(p.astype(BF), v_ref[:, cs]).astype(BF)
    o_ref[...] = h + _dot(att_s[...], wo_ref[...])


def _xattn(h3, kx, vx, g, q_norm, w_q, w_o, tm):
    b, s, _ = h3.shape
    m = kx.shape[1]
    tile = lambda: pl.BlockSpec((None, tm, D_MODEL), lambda bi, i: (bi, i, 0))
    kv = lambda: pl.BlockSpec((None, m, D_MODEL), lambda bi, i: (bi, 0, 0))
    return pl.pallas_call(
        _xattn_body,
        grid=(b, s // tm),
        in_specs=[tile(), kv(), kv(), _const_spec((1, D_MODEL)), _const_spec((1, X_DH)),
                  _const_spec((D_MODEL, D_MODEL)), _const_spec((D_MODEL, D_MODEL))],
        out_specs=tile(),
        out_shape=jax.ShapeDtypeStruct((b, s, D_MODEL), F32),
        scratch_shapes=[pltpu.VMEM((tm, D_MODEL), BF)],
        compiler_params=_params("parallel", "parallel"),
        name="xattn",
    )(h3, kx, vx, g, q_norm, w_q, w_o)


def _ffn_body(h_ref, hp_ref, g_ref, wu_ref, cw_ref, wd_ref, o_ref, hx_s, act_s):
    h = h_ref[...]
    g = g_ref[...]
    first = pl.program_id(1) == 0
    hx_s[0:FFN_HALO, :] = jnp.where(first, 0.0, _rms(hp_ref[...], g)).astype(BF)
    hx_s[FFN_HALO:, :] = _rms(h, g).astype(BF)

    def up_conv(c0):
        a = _dot(hx_s[...], wu_ref[:, c0:c0 + FFN_FC])
        cw = cw_ref[:, c0:c0 + FFN_FC]
        y = a[FFN_HALO:, :] * cw[FFN_CONV - 1:FFN_CONV, :]
        for i in range(FFN_CONV - 1):
            sh = FFN_CONV - 1 - i
            y = y + pltpu.roll(a, sh, axis=0)[FFN_HALO:, :] * cw[i:i + 1, :]
        return y

    for f in range(D_FF // FFN_FC):
        ua = up_conv(f * FFN_FC)
        ug = up_conv(D_FF + f * FFN_FC)
        act_s[:, f * FFN_FC:(f + 1) * FFN_FC] = (ua * _sigmoid(ua) * ug).astype(BF)
    o_ref[...] = h + _dot(act_s[...], wd_ref[...])


def _ffn(h3, g, w_up, cw, w_down, tm):
    b, s, _ = h3.shape
    halo_blocks = tm // FFN_HALO
    return pl.pallas_call(
        _ffn_body,
        grid=(b, s // tm),
        in_specs=[
            pl.BlockSpec((None, tm, D_MODEL), lambda bi, i: (bi, i, 0)),
            pl.BlockSpec((None, FFN_HALO, D_MODEL),
                         lambda bi, i: (bi, jnp.maximum(i * halo_blocks - 1, 0), 0)),
            _const_spec((1, D_MODEL)),
            _const_spec((D_MODEL, 2 * D_FF)),
            _const_spec((FFN_CONV, 2 * D_FF)),
            _const_spec((D_FF, D_MODEL)),
        ],
        out_specs=pl.BlockSpec((None, tm, D_MODEL), lambda bi, i: (bi, i, 0)),
        out_shape=jax.ShapeDtypeStruct((b, s, D_MODEL), F32),
        scratch_shapes=[pltpu.VMEM((tm + FFN_HALO, D_MODEL), BF), pltpu.VMEM((tm, D_FF), BF)],
        compiler_params=_params("parallel", "arbitrary"),
        name="ffn",
    )(h3, h3, g, w_up, cw, w_down)


def _pick(n, pref):
    while n % pref:
        pref //= 2
    return pref


def _layer(h, mem, norm_mix, w_in, conv_gdn, a_log, dt_bias, gdn_out_norm, w_proj_gdn, w_proj_sb, w_out,
           norm_x, norm_mem, w_xq, w_xkv, xq_norm, xk_norm, w_xo, norm_ffn, w_up, conv_ffn, w_down):
    b, s, d = h.shape
    t = b * s
    assert d == D_MODEL and s % TILE == 0
    qkv_w = 3 * HEADS * HEAD_DIM
    ab_end = qkv_w + 2 * HEADS
    heads_end = ab_end + 4 * HEADS * HEAD_DIM

    w_heads = jnp.concatenate([w_in[:, :qkv_w], w_in[:, ab_end:heads_end]], axis=1).astype(BF)
    w_ab = jnp.pad(w_in[:, qkv_w:ab_end], ((0, 0), (0, HEAD_DIM - 2 * HEADS))).astype(BF)
    w_gates = w_in[:, heads_end:].astype(BF)
    pad_row = lambda v: jnp.pad(v, (0, HEAD_DIM - HEADS))[None, :]

    x2 = h.reshape(t, d)
    heads, xn, ab = _in_proj(x2, norm_mix[None, :], w_heads, w_ab, _pick(t, 1024))
    heads4 = heads.reshape(N_HEAD_BLOCKS, b, s, HEAD_DIM)
    o_a = _gdn(heads4, ab.reshape(b, s, HEAD_DIM), conv_gdn, pad_row(a_log), pad_row(dt_bias),
               gdn_out_norm[None, :])
    o_b = _sb(heads4, _pick(s, SB_BLK))
    h1 = _merge_out(o_a.reshape(t, d), o_b.reshape(t, d), xn, x2, w_gates,
                    w_proj_gdn.astype(BF), w_proj_sb.astype(BF), w_out.astype(BF), _pick(t, 512))
    kx, vx = _mem_kv(mem, norm_mem[None, :], w_xkv.astype(BF), xk_norm[None, :])
    h2 = _xattn(h1.reshape(b, s, d), kx, vx, norm_x[None, :], xq_norm[None, :],
                w_xq.astype(BF), w_xo.astype(BF), _pick(s, 512))
    return _ffn(h2, norm_ffn[None, :], w_up.astype(BF), conv_ffn, w_down.astype(BF), _pick(s, 512))


def kernel(x, mem, norm_mix, w_in, conv_gdn, a_log, dt_bias, gdn_out_norm, w_proj_gdn, w_proj_sb, w_out,
           norm_x, norm_mem, w_xq, w_xkv, xq_norm, xk_norm, w_xo, norm_ffn, w_up, conv_ffn, w_down):
    h = x
    for l in range(norm_mix.shape[0]):
        h = _layer(h, mem, norm_mix[l], w_in[l], conv_gdn[l], a_log[l], dt_bias[l], gdn_out_norm[l],
                   w_proj_gdn[l], w_proj_sb[l], w_out[l], norm_x[l], norm_mem[l], w_xq[l], w_xkv[l],
                   xq_norm[l], xk_norm[l], w_xo[l], norm_ffn[l], w_up[l], conv_ffn[l], w_down[l])
    return h
```

```python
import functools

import jax
import jax.numpy as jnp
from jax import lax
from jax.experimental import pallas as pl
from jax.experimental.pallas import tpu as pltpu

F32 = jnp.float32
BF = jnp.bfloat16

D_MODEL = 1024
HEADS = 8
HEAD_DIM = 128
TILE = 128
GDN_CONV = 4
GDN_PREP_TILES = 16
SB_TK = 128
SB_BLK = 512
X_HEADS = 4
X_DH = D_MODEL // X_HEADS
D_FF = 2816
FFN_CONV = 3
FFN_FC = 256
FFN_HALO = 16
EPS = 1e-6
LOG2E = 1.4426950408889634
VMEM_LIMIT = 56 * 1024 * 1024

_GQ, _GK, _GV, _GZ, _SQ, _SK, _SV = (i * HEADS for i in range(7))
N_HEAD_BLOCKS = 7 * HEADS


def _dot(a, b):
    return jnp.dot(a, b, preferred_element_type=F32)


def _dot_nt(a, b):
    return lax.dot_general(a, b, (((1,), (1,)), ((), ())), preferred_element_type=F32)


def _bdot(a, b):
    return _dot(a.astype(BF), b.astype(BF))


def _split(a):
    hi = a.astype(BF)
    lo = (a - hi.astype(F32)).astype(BF)
    return hi, lo


def _rms(x, g):
    return x * lax.rsqrt(jnp.mean(x * x, axis=-1, keepdims=True) + EPS) * g


def _softplus(x):
    return jnp.maximum(x, 0.0) + jnp.log1p(jnp.exp(-jnp.abs(x)))


def _sigmoid(x):
    return 1.0 / (1.0 + jnp.exp(-x))


def _params(*sem):
    return pltpu.CompilerParams(dimension_semantics=sem, vmem_limit_bytes=VMEM_LIMIT)


def _const_spec(shape):
    n = len(shape)
    return pl.BlockSpec(shape, lambda *_: (0,) * n, pipeline_mode=pl.Buffered(1))


def _in_proj_body(x_ref, g_ref, w_ref, wab_ref, heads_ref, xn_ref, ab_ref):
    @pl.when(pl.program_id(1) == 0)
    def _():
        xn = _rms(x_ref[...], g_ref[...]).astype(BF)
        xn_ref[...] = xn
        ab_ref[...] = _dot(xn, wab_ref[...])

    res = _dot(xn_ref[...], w_ref[...])
    for c in range(HEADS):
        heads_ref[c] = res[:, c * HEAD_DIM:(c + 1) * HEAD_DIM].astype(BF)


def _in_proj(x2, g, w_heads, w_ab, tm):
    t = x2.shape[0]
    nj = N_HEAD_BLOCKS // HEADS
    wn = HEADS * HEAD_DIM
    return pl.pallas_call(
        _in_proj_body,
        grid=(t // tm, nj),
        in_specs=[
            pl.BlockSpec((tm, D_MODEL), lambda i, j: (i, 0)),
            _const_spec((1, D_MODEL)),
            pl.BlockSpec((D_MODEL, wn), lambda i, j: (0, j)),
            _const_spec((D_MODEL, HEAD_DIM)),
        ],
        out_specs=[
            pl.BlockSpec((HEADS, tm, HEAD_DIM), lambda i, j: (j, i, 0)),
            pl.BlockSpec((tm, D_MODEL), lambda i, j: (i, 0)),
            pl.BlockSpec((tm, HEAD_DIM), lambda i, j: (i, 0)),
        ],
        out_shape=[
            jax.ShapeDtypeStruct((N_HEAD_BLOCKS, t, HEAD_DIM), BF),
            jax.ShapeDtypeStruct((t, D_MODEL), BF),
            jax.ShapeDtypeStruct((t, HEAD_DIM), F32),
        ],
        compiler_params=_params("parallel", "arbitrary"),
        name="in_proj",
    )(x2, g, w_heads, w_ab)


def _gdn_body(q_ref, k_ref, v_ref, z_ref, ab_ref, cq_ref, ck_ref, cv_ref, alog_ref, dt_ref, onorm_ref,
              o_ref,
              gsplit_s, qb_s, qd_s, kn_s, vb_s, gc_s, beta_s, u_s, w_s, qk_s, a_s, b_s, cd_s):
    s_len = q_ref.shape[0]
    n_tiles = s_len // TILE
    h = pl.program_id(1)

    def conv_silu(x_ref, cw_ref):
        x = x_ref[...].astype(F32)
        cw = cw_ref[...]
        xz = jnp.concatenate([jnp.zeros((8, HEAD_DIM), F32), x], axis=0)
        acc = x * cw[GDN_CONV - 1:GDN_CONV, :]
        for i in range(GDN_CONV - 1):
            acc = acc + pltpu.roll(xz, GDN_CONV - 1 - i, axis=0)[8:, :] * cw[i:i + 1, :]
        return acc * _sigmoid(acc)

    def l2n(x):
        return x * lax.rsqrt(jnp.sum(x * x, axis=-1, keepdims=True) + EPS)

    @pl.when(h == 0)
    def _():
        rows = lax.broadcasted_iota(jnp.int32, (s_len, HEAD_DIM), 0)
        lanes = lax.broadcasted_iota(jnp.int32, (s_len, HEAD_DIM), 1)
        ab = ab_ref[...]
        g_all = -jnp.exp(alog_ref[...]) * _softplus(ab + dt_ref[...])
        pos = rows % TILE
        sh = 1
        while sh < TILE:
            g_all = g_all + jnp.where(pos >= sh, pltpu.roll(g_all, sh, axis=0), 0.0)
            sh *= 2
        x = jnp.where(lanes < HEADS, g_all, ab)
        for piece in range(3):
            xb = x.astype(BF)
            gsplit_s[:, piece * HEAD_DIM:(piece + 1) * HEAD_DIM] = xb
            x = x - xb.astype(F32)

    sel_r = lax.broadcasted_iota(jnp.int32, (3 * HEAD_DIM, 2 * HEAD_DIM), 0) % HEAD_DIM
    sel_c = lax.broadcasted_iota(jnp.int32, (3 * HEAD_DIM, 2 * HEAD_DIM), 1)
    onehot = (sel_r == jnp.where(sel_c < HEAD_DIM, h, h + HEADS)).astype(BF)
    sel = _dot(gsplit_s[...], onehot)
    gcb = sel[:, :HEAD_DIM]
    beta = _sigmoid(sel[:, HEAD_DIM:])
    gc_s[...] = gcb
    beta_s[...] = beta

    qn = l2n(conv_silu(q_ref, cq_ref)) * (HEAD_DIM ** -0.5)
    qb_s[...] = qn.astype(BF)
    qd_s[...] = (qn * jnp.exp(gcb)).astype(BF)
    kn_s[...] = l2n(conv_silu(k_ref, ck_ref))
    vb_s[...] = (conv_silu(v_ref, cv_ref) * beta).astype(BF)

    ri = lax.broadcasted_iota(jnp.int32, (TILE, TILE), 0)
    ci = lax.broadcasted_iota(jnp.int32, (TILE, TILE), 1)
    incl = ri >= ci
    strict = ri > ci
    eye = (ri == ci).astype(F32)

    def prep_tiles(ts):
        each = lambda f, *xs: [f(*a) for a in zip(*xs)]
        sls = [pl.ds(pl.multiple_of(t * TILE, TILE), TILE) for t in ts]
        k = [kn_s[sl, :] for sl in sls]
        gcb = [gc_s[sl, :] for sl in sls]
        bb = [beta_s[sl, :] for sl in sls]
        vb = [vb_s[sl, :] for sl in sls]
        qb = [qb_s[sl, :] for sl in sls]

        decay = each(lambda g: jnp.where(incl, jnp.exp(jnp.where(incl, g - g.T, 0.0)), 0.0), gcb)
        kb = each(lambda x: x.astype(BF), k)
        kk = each(_dot_nt, kb, kb)
        m = each(lambda b_, kk_, d_: jnp.where(strict, b_ * kk_ * d_, 0.0), bb, kk, decay)

        base = 16
        base_blk = (ri // base) == (ci // base)
        mp = each(lambda m_: jnp.where(base_blk, m_, 0.0), m)
        p = each(lambda mp_: eye - mp_, mp)
        width = 2
        while width < base:
            mp = each(_bdot, mp, mp)
            p = each(lambda p_, mp_: p_ + _bdot(p_, mp_), p, mp)
            width *= 2
        size = base
        while size < TILE:
            off =((ri // (2 * size)) == (ci // (2 * size))) & ((ri // size) != (ci // size))
            pc = each(lambda p_, m_: _bdot(p_, jnp.where(off, m_, 0.0)), p, m)
            p = each(lambda p_, pc_: p_ - _bdot(pc_, p_), p, pc)
            size *= 2

        rhs = each(lambda vb_, k_, b_, g_: jnp.concatenate([vb_, (k_ * (b_ * jnp.exp(g_))).astype(BF)], axis=1),
                   vb, k, bb, gcb)
        sol = each(lambda p_, r_: _dot(p_.astype(BF), r_), p, rhs)
        qk = each(lambda q_, kb_, d_: (_dot_nt(q_, kb_) * d_).astype(BF), qb, kb, decay)

        gl = [g[TILE - 1:TILE, :] for g in gcb]
        kdt = each(lambda k_, g_, l_: (k_ * jnp.exp(l_ - g_)).T.astype(BF), k, gcb, gl)
        uw = each(lambda s_: jnp.concatenate([s_[:, :HEAD_DIM], -s_[:, HEAD_DIM:]], axis=1).astype(BF), sol)
        ba = each(_dot, kdt, uw)

        for i, t in enumerate(ts):
            u_s[sls[i], :] = sol[i][:, :HEAD_DIM]
            w_s[sls[i], :] = sol[i][:, HEAD_DIM:].astype(BF)
            qk_s[t] = qk[i]
            b_s[t] = ba[i][:, :HEAD_DIM]
            a_s[t] = ba[i][:, HEAD_DIM:].astype(BF)
            cd_s[t, 0:1, :] = jnp.exp(gl[i])

    prep_unroll = _pick(n_tiles, GDN_PREP_TILES)

    def prep_body(tt, carry):
        prep_tiles([tt * prep_unroll + j for j in range(prep_unroll)])
        return carry

    lax.fori_loop(0, n_tiles // prep_unroll, prep_body, 0)

    def scan_step(t, carry, on):
        state, sb, vn, oq, o = carry
        new_sb, new_vn, new_oq, new_o = sb, vn, oq, o
        if on[0]:
            new_sb = state.astype(BF)
            nxt = _dot(a_s[t], new_sb)
        if on[1]:
            sl = pl.ds(pl.multiple_of((t - 1) * TILE, TILE), TILE)
            new_vn = (u_s[sl, :] - _dot(w_s[sl, :], sb)).astype(BF)
            new_oq = _dot(qd_s[sl, :], sb)
        if on[2]:
            new_o = oq + _dot(qk_s[t - 2], vn)
        if on[0]:
            state = state * cd_s[t, 0:1, :] + nxt + b_s[t]
        if on[3]:
            sl = pl.ds(pl.multiple_of((t - 3) * TILE, TILE), TILE)
            z = z_ref[sl, :].astype(F32)
            o_ref[sl, :] = (_rms(o, onorm_ref[...]) * (z * _sigmoid(z))).astype(BF)
        return state, new_sb, new_vn, new_oq, new_o

    depth = 4
    live = lambda t: tuple(0 <= t - k < n_tiles for k in range(depth))
    carry = (jnp.zeros((HEAD_DIM, HEAD_DIM), F32), None, None, None, None)
    for t in range(depth - 1):
        carry = scan_step(t, carry, live(t))
    if n_tiles > depth - 1:
        carry = lax.fori_loop(depth - 1, n_tiles, lambda t, c: scan_step(t, c, (True,) * depth), carry)
    for t in range(max(depth - 1, n_tiles), n_tiles + depth - 1):
        carry = scan_step(t, carry, live(t))


def _gdn(heads4, ab3, cw, alog_row, dt_row, onorm_row):
    _, b, s, _ = heads4.shape
    n_tiles = s // TILE

    def head_spec(base):
        return pl.BlockSpec((None, None, s, HEAD_DIM), lambda bi, hi: (base + hi, bi, 0, 0))

    def conv_spec(base):
        return pl.BlockSpec((GDN_CONV, HEAD_DIM), lambda bi, hi: (0, base + hi))

    row_spec = pl.BlockSpec((1, HEAD_DIM), lambda bi, hi: (0, 0))
    seq_f32 = pltpu.VMEM((s, HEAD_DIM), F32)
    seq_bf = pltpu.VMEM((s, HEAD_DIM), BF)
    return pl.pallas_call(
        _gdn_body,
        grid=(b, HEADS),
        in_specs=[head_spec(_GQ), head_spec(_GK), head_spec(_GV), head_spec(_GZ),
                  pl.BlockSpec((None, s, HEAD_DIM), lambda bi, hi: (bi, 0, 0)),
                  conv_spec(0), conv_spec(HEADS), conv_spec(2 * HEADS),
                  row_spec, row_spec, row_spec],
        out_specs=pl.BlockSpec((None, s, HEAD_DIM), lambda bi, hi: (bi, 0, hi)),
        out_shape=jax.ShapeDtypeStruct((b, s, HEADS * HEAD_DIM), BF),
        scratch_shapes=[pltpu.VMEM((s, 3 * HEAD_DIM), BF),
                        seq_bf, seq_bf, seq_f32, seq_bf,
                        seq_f32, seq_f32,
                        seq_f32, seq_bf,
                        pltpu.VMEM((n_tiles, TILE, TILE), BF),
                        pltpu.VMEM((n_tiles, HEAD_DIM, HEAD_DIM), BF),
                        pltpu.VMEM((n_tiles, HEAD_DIM, HEAD_DIM), F32),
                        pltpu.VMEM((n_tiles, 8, HEAD_DIM), F32)],
        compiler_params=_params("parallel", "arbitrary"),
        name="gdn",
    )(heads4, heads4, heads4, heads4, ab3, cw, cw, cw, alog_row, dt_row, onorm_row)


def _neg_abs(x):
    sign = jnp.uint32(0x80000000)
    return lax.bitcast_convert_type(lax.bitcast_convert_type(x, jnp.uint32) | sign, F32)


def _sb_body(q_ref, k_ref, v_ref, r2_ref, o_ref, *, blk):
    s_len = q_ref.shape[0]
    n_sub = blk // SB_TK
    ri = lax.broadcasted_iota(jnp.int32, (blk, blk), 0)
    ci = lax.broadcasted_iota(jnp.int32, (blk, blk), 1)
    causal = ci < ri

    def step(q, jb, carry, masked):
        acc, c = carry
        ks_ = slice(jb * blk, (jb + 1) * blk)
        z2 = _dot_nt(q, k_ref[ks_, :])
        sp = jnp.maximum(z2, 0.0) + jnp.log2(1.0 + jnp.exp2(_neg_abs(z2)))
        if masked:
            sp = jnp.where(causal, sp, 0.0)
        hi, lo = _split(sp)
        wparts = [None] * n_sub
        for m in reversed(range(n_sub)):
            cs = slice(m * SB_TK, (m + 1) * SB_TK)
            sr = _dot(jnp.concatenate([hi[:, cs], lo[:, cs]], axis=1), r2_ref[...])
            w = jnp.exp2((z2[:, cs] - sp[:, cs]) + sr[:, :SB_TK] + c)
            if masked:
                w = jnp.where(causal[:, cs], w, 0.0)
            wparts[m] = w.astype(BF)
            c = c + sr[:, SB_TK:]
        return acc + _dot(jnp.concatenate(wparts, axis=1), v_ref[ks_, :]), c

    for qi in range(s_len // blk):
        qs = slice(qi * blk, (qi + 1) * blk)
        q = (q_ref[qs, :].astype(F32) * (HEAD_DIM ** -0.5 * LOG2E)).astype(BF)
        carry = (jnp.zeros((blk, HEAD_DIM), F32), jnp.zeros((blk, SB_TK), F32))
        for jb in range(qi, -1, -1):
            carry = step(q, jb, carry, jb == qi)
        o_ref[qs, :] = carry[0].astype(BF)


def _sb(heads4, blk):
    _, b, s, _ = heads4.shape
    assert s % blk == 0 and blk % SB_TK == 0
    jr = jnp.arange(2 * SB_TK)[:, None] % SB_TK
    jc = jnp.arange(2 * SB_TK)[None, :]
    r2 = -jnp.where(jc < SB_TK, jr > jc, True).astype(BF)
    seq = lambda base: pl.BlockSpec((None, None, s, HEAD_DIM), lambda bi, hi: (base + hi, bi, 0, 0))
    return pl.pallas_call(
        functools.partial(_sb_body, blk=blk),
        grid=(b, HEADS),
        in_specs=[seq(_SQ), seq(_SK), seq(_SV), _const_spec((2 * SB_TK, 2 * SB_TK))],
        out_specs=pl.BlockSpec((None, s, HEAD_DIM), lambda bi, hi: (bi, 0, hi)),
        out_shape=jax.ShapeDtypeStruct((b, s, HEADS * HEAD_DIM), BF),
        compiler_params=_params("parallel", "parallel"),
        name="sb",
    )(heads4, heads4, heads4, r2)


def _merge_body(oa_ref, ob_ref, xn_ref, x_ref, wg_ref, wpa_ref, wpb_ref, wo_ref, h_ref):
    gates = _dot(xn_ref[...], wg_ref[...])
    pa = _dot(oa_ref[...], wpa_ref[...])
    pb = _dot(ob_ref[...], wpb_ref[...])
    merged = _sigmoid(gates[:, :D_MODEL]) * pa + _sigmoid(gates[:, D_MODEL:]) * pb
    h_ref[...] = x_ref[...] + _dot(merged.astype(BF), wo_ref[...])


def _merge_out(oa, ob, xn, x2, w_gates, w_pa, w_pb, w_o, tm):
    t = x2.shape[0]
    tile = lambda: pl.BlockSpec((tm, D_MODEL), lambda i: (i, 0))
    return pl.pallas_call(
        _merge_body,
        grid=(t // tm,),
        in_specs=[tile(), tile(), tile(), tile(),
                  _const_spec((D_MODEL, 2 * D_MODEL)), _const_spec((D_MODEL, D_MODEL)),
                  _const_spec((D_MODEL, D_MODEL)), _const_spec((D_MODEL, D_MODEL))],
        out_specs=tile(),
        out_shape=jax.ShapeDtypeStruct((t, D_MODEL), F32),
        compiler_params=_params("parallel"),
        name="merge_out",
    )(oa, ob, xn, x2, w_gates, w_pa, w_pb, w_o)


def _mem_kv_body(mem_ref, g_ref, w_ref, kn_ref, k_ref, v_ref):
    mn = _rms(mem_ref[...], g_ref[...]).astype(BF)
    kv = _dot(mn, w_ref[...])
    for hh in range(X_HEADS):
        cs = slice(hh * X_DH, (hh + 1) * X_DH)
        k_ref[:, cs] = _rms(kv[:, cs], kn_ref[...]).astype(BF)
    v_ref[...] = kv[:, D_MODEL:].astype(BF)


def _mem_kv(mem, g, w_xkv, k_norm):
    b, m, _ = mem.shape
    blk = lambda: pl.BlockSpec((None, m, D_MODEL), lambda bi: (bi, 0, 0))
    return pl.pallas_call(
        _mem_kv_body,
        grid=(b,),
        in_specs=[blk(), _const_spec((1, D_MODEL)), _const_spec((D_MODEL, 2 * D_MODEL)),
                  _const_spec((1, X_DH))],
        out_specs=[blk(), blk()],
        out_shape=[jax.ShapeDtypeStruct((b, m, D_MODEL), BF)] * 2,
        compiler_params=_params("parallel"),
        name="mem_kv",
    )(mem, g, w_xkv, k_norm)


def _xattn_body(h_ref, k_ref, v_ref, g_ref, qn_ref, wq_ref, wo_ref, o_ref, att_s):
    h = h_ref[...]
    q = _dot(_rms(h, g_ref[...]).astype(BF), wq_ref[...])
    for hh in range(X_HEADS):
        cs = slice(hh * X_DH, (hh + 1) * X_DH)
        qh = _rms(q[:, cs], qn_ref[...]).astype(BF)
        sc = _dot_nt(qh, k_ref[:, cs]) * (X_DH ** -0.5)
        e = jnp.exp(sc - jnp.max(sc, axis=-1, keepdims=True))
        p = e / jnp.sum(e, axis=-1, keepdims=True)
        att_s[:, cs] = _dot(p.astype(BF), v_ref[:, cs]).astype(BF)
    o_ref[...] = h + _dot(att_s[...], wo_ref[...])


def _xattn(h3, kx, vx, g, q_norm, w_q, w_o, tm):
    b, s, _ = h3.shape
    m = kx.shape[1]
    tile = lambda: pl.BlockSpec((None, tm, D_MODEL), lambda bi, i: (bi, i, 0))
    kv = lambda: pl.BlockSpec((None, m, D_MODEL), lambda bi, i: (bi, 0, 0))
    return pl.pallas_call(
        _xattn_body,
        grid=(b, s // tm),
        in_specs=[tile(), kv(), kv(), _const_spec((1, D_MODEL)), _const_spec((1, X_DH)),
                  _const_spec((D_MODEL, D_MODEL)), _const_spec((D_MODEL, D_MODEL))],
        out_specs=tile(),
        out_shape=jax.ShapeDtypeStruct((b, s, D_MODEL), F32),
        scratch_shapes=[pltpu.VMEM((tm, D_MODEL), BF)],
        compiler_params=_params("parallel", "parallel"),
        name="xattn",
    )(h3, kx, vx, g, q_norm, w_q, w_o)


def _ffn_body(h_ref, hp_ref, g_ref, wu_ref, cw_ref, wd_ref, o_ref, hx_s, act_s):
    h = h_ref[...]
    g = g_ref[...]
    first = pl.program_id(1) == 0
    hx_s[0:FFN_HALO, :] = jnp.where(first, 0.0, _rms(hp_ref[...], g)).astype(BF)
    hx_s[FFN_HALO:, :] = _rms(h, g).astype(BF)

    def up_conv(c0):
        a = _dot(hx_s[...], wu_ref[:, c0:c0 + FFN_FC])
        cw = cw_ref[:, c0:c0 + FFN_FC]
        y = a[FFN_HALO:, :] * cw[FFN_CONV - 1:FFN_CONV, :]
        for i in range(FFN_CONV - 1):
            sh = FFN_CONV - 1 - i
            y = y + pltpu.roll(a, sh, axis=0)[FFN_HALO:, :] * cw[i:i + 1, :]
        return y

    for f in range(D_FF // FFN_FC):
        ua = up_conv(f * FFN_FC)
        ug = up_conv(D_FF + f * FFN_FC)
        act_s[:, f * FFN_FC:(f + 1) * FFN_FC] = (ua * _sigmoid(ua) * ug).astype(BF)
    o_ref[...] = h + _dot(act_s[...], wd_ref[...])


def _ffn(h3, g, w_up, cw, w_down, tm):
    b, s, _ = h3.shape
    halo_blocks = tm // FFN_HALO
    return pl.pallas_call(
        _ffn_body,
        grid=(b, s // tm),
        in_specs=[
            pl.BlockSpec((None, tm, D_MODEL), lambda bi, i: (bi, i, 0)),
            pl.BlockSpec((None, FFN_HALO, D_MODEL),
                         lambda bi, i: (bi, jnp.maximum(i * halo_blocks - 1, 0), 0)),
            _const_spec((1, D_MODEL)),
            _const_spec((D_MODEL, 2 * D_FF)),
            _const_spec((FFN_CONV, 2 * D_FF)),
            _const_spec((D_FF, D_MODEL)),
        ],
        out_specs=pl.BlockSpec((None, tm, D_MODEL), lambda bi, i: (bi, i, 0)),
        out_shape=jax.ShapeDtypeStruct((b, s, D_MODEL), F32),
        scratch_shapes=[pltpu.VMEM((tm + FFN_HALO, D_MODEL), BF), pltpu.VMEM((tm, D_FF), BF)],
        compiler_params=_params("parallel", "arbitrary"),
        name="ffn",
    )(h3, h3, g, w_up, cw, w_down)


def _pick(n, pref):
    while n % pref:
        pref //= 2
    return pref


def _layer(h, mem, norm_mix, w_in, conv_gdn, a_log, dt_bias, gdn_out_norm, w_proj_gdn, w_proj_sb, w_out,
           norm_x, norm_mem, w_xq, w_xkv, xq_norm, xk_norm, w_xo, norm_ffn, w_up, conv_ffn, w_down):
    b, s, d = h.shape
    t = b * s
    assert d == D_MODEL and s % TILE == 0
    qkv_w = 3 * HEADS * HEAD_DIM
    ab_end = qkv_w + 2 * HEADS
    heads_end = ab_end + 4 * HEADS * HEAD_DIM

    w_heads = jnp.concatenate([w_in[:, :qkv_w], w_in[:, ab_end:heads_end]], axis=1).astype(BF)
    w_ab = jnp.pad(w_in[:, qkv_w:ab_end], ((0, 0), (0, HEAD_DIM - 2 * HEADS))).astype(BF)
    w_gates = w_in[:, heads_end:].astype(BF)
    pad_row = lambda v: jnp.pad(v, (0, HEAD_DIM - HEADS))[None, :]

    x2 = h.reshape(t, d)
    heads, xn, ab = _in_proj(x2, norm_mix[None, :], w_heads, w_ab, _pick(t, 2048))
    heads4 = heads.reshape(N_HEAD_BLOCKS, b, s, HEAD_DIM)
    o_a = _gdn(heads4, ab.reshape(b, s, HEAD_DIM), conv_gdn, pad_row(a_log), pad_row(dt_bias),
               gdn_out_norm[None, :])
    o_b = _sb(heads4, _pick(s, SB_BLK))
    h1 = _merge_out(o_a.reshape(t, d), o_b.reshape(t, d), xn, x2, w_gates,
                    w_proj_gdn.astype(BF), w_proj_sb.astype(BF), w_out.astype(BF), _pick(t, 512))
    kx, vx = _mem_kv(mem, norm_mem[None, :], w_xkv.astype(BF), xk_norm[None, :])
    h2 = _xattn(h1.reshape(b, s, d), kx, vx, norm_x[None, :], xq_norm[None, :],
                w_xq.astype(BF), w_xo.astype(BF), _pick(s, 512))
    return _ffn(h2, norm_ffn[None, :], w_up.astype(BF), conv_ffn, w_down.astype(BF), _pick(s, 512))


def kernel(x, mem, norm_mix, w_in, conv_gdn, a_log, dt_bias, gdn_out_norm, w_proj_gdn, w_proj_sb, w_out,
           norm_x, norm_mem, w_xq, w_xkv, xq_norm, xk_norm, w_xo, norm_ffn, w_up, conv_ffn, w_down):
    h = x
    for l in range(norm_mix.shape[0]):
        h = _layer(h, mem, norm_mix[l], w_in[l], conv_gdn[l], a_log[l], dt_bias[l], gdn_out_norm[l],
                   w_proj_gdn[l], w_proj_sb[l], w_out[l], norm_x[l], norm_mem[l], w_xq[l], w_xkv[l],
                   xq_norm[l], xk_norm[l], w_xo[l], norm_ffn[l], w_up[l], conv_ffn[l], w_down[l])
    return h
```

```python
import functools

import jax
import jax.numpy as jnp
from jax import lax
from jax.experimental import pallas as pl
from jax.experimental.pallas import tpu as pltpu

F32 = jnp.float32
BF = jnp.bfloat16

D_MODEL = 1024
HEADS = 8
HEAD_DIM = 128
TILE = 128
GDN_CONV = 4
GDN_WAVES = 1
GDN_SCAN_STRIDE = 2
SB_TK = 128
SB_BLK = 512
X_HEADS = 4
X_DH = D_MODEL // X_HEADS
D_FF = 2816
FFN_CONV = 3
FFN_FC = 256
FFN_HALO = 16
EPS = 1e-6
LOG2E = 1.4426950408889634
VMEM_LIMIT = 56 * 1024 * 1024

_GQ, _GK, _GV, _GZ, _SQ, _SK, _SV = (i * HEADS for i in range(7))
N_HEAD_BLOCKS = 7 * HEADS


def _dot(a, b):
    return jnp.dot(a, b, preferred_element_type=F32)


def _dot_nt(a, b):
    return lax.dot_general(a, b, (((1,), (1,)), ((), ())), preferred_element_type=F32)


def _bdot(a, b):
    return _dot(a.astype(BF), b.astype(BF))


def _split(a):
    hi = a.astype(BF)
    lo = (a - hi.astype(F32)).astype(BF)
    return hi, lo


def _rms(x, g):
    return x * lax.rsqrt(jnp.mean(x * x, axis=-1, keepdims=True) + EPS) * g


def _softplus(x):
    return jnp.maximum(x, 0.0) + jnp.log1p(jnp.exp(-jnp.abs(x)))


def _sigmoid(x):
    return 1.0 / (1.0 + jnp.exp(-x))


def _params(*sem):
    return pltpu.CompilerParams(dimension_semantics=sem, vmem_limit_bytes=VMEM_LIMIT)


def _const_spec(shape):
    n = len(shape)
    return pl.BlockSpec(shape, lambda *_: (0,) * n, pipeline_mode=pl.Buffered(1))


def _in_proj_body(x_ref, g_ref, w_ref, wab_ref, heads_ref, xn_ref, ab_ref):
    @pl.when(pl.program_id(1) == 0)
    def _():
        xn = _rms(x_ref[...], g_ref[...]).astype(BF)
        xn_ref[...] = xn
        ab_ref[...] = _dot(xn, wab_ref[...])

    res = _dot(xn_ref[...], w_ref[...])
    for c in range(HEADS):
        heads_ref[c] = res[:, c * HEAD_DIM:(c + 1) * HEAD_DIM].astype(BF)


def _in_proj(x2, g, w_heads, w_ab, tm):
    t = x2.shape[0]
    nj = N_HEAD_BLOCKS // HEADS
    wn = HEADS * HEAD_DIM
    return pl.pallas_call(
        _in_proj_body,
        grid=(t // tm, nj),
        in_specs=[
            pl.BlockSpec((tm, D_MODEL), lambda i, j: (i, 0)),
            _const_spec((1, D_MODEL)),
            pl.BlockSpec((D_MODEL, wn), lambda i, j: (0, j)),
            _const_spec((D_MODEL, HEAD_DIM)),
        ],
        out_specs=[
            pl.BlockSpec((HEADS, tm, HEAD_DIM), lambda i, j: (j, i, 0)),
            pl.BlockSpec((tm, D_MODEL), lambda i, j: (i, 0)),
            pl.BlockSpec((tm, HEAD_DIM), lambda i, j: (i, 0)),
        ],
        out_shape=[
            jax.ShapeDtypeStruct((N_HEAD_BLOCKS, t, HEAD_DIM), BF),
            jax.ShapeDtypeStruct((t, D_MODEL), BF),
            jax.ShapeDtypeStruct((t, HEAD_DIM), F32),
        ],
        compiler_params=_params("parallel", "arbitrary"),
        name="in_proj",
    )(x2, g, w_heads, w_ab)


def _gdn_trace(q_ref, k_ref, v_ref, z_ref, ab_ref, cq_ref, ck_ref, cv_ref, alog_ref, dt_ref, onorm_ref,
               o_ref,
               gsplit_s, qb_s, qd_s, kn_s, vb_s, gc_s, beta_s, u_s, w_s, qk_s, a_s, b_s, cd_s, other):
    s_len = q_ref.shape[0]
    n_tiles = s_len // TILE
    h = pl.program_id(1)

    def conv_silu(x_ref, cw_ref):
        x = x_ref[...].astype(F32)
        cw = cw_ref[...]
        xz = jnp.concatenate([jnp.zeros((8, HEAD_DIM), F32), x], axis=0)
        acc = x * cw[GDN_CONV - 1:GDN_CONV, :]
        for i in range(GDN_CONV - 1):
            acc = acc + pltpu.roll(xz, GDN_CONV - 1 - i, axis=0)[8:, :] * cw[i:i + 1, :]
        return acc * _sigmoid(acc)

    def l2n(x):
        return x * lax.rsqrt(jnp.sum(x * x, axis=-1, keepdims=True) + EPS)

    @pl.when(h == 0)
    def _():
        rows = lax.broadcasted_iota(jnp.int32, (s_len, HEAD_DIM), 0)
        lanes = lax.broadcasted_iota(jnp.int32, (s_len, HEAD_DIM), 1)
        ab = ab_ref[...]
        g_all = -jnp.exp(alog_ref[...]) * _softplus(ab + dt_ref[...])
        pos = rows % TILE
        sh = 1
        while sh < TILE:
            g_all = g_all + jnp.where(pos >= sh, pltpu.roll(g_all, sh, axis=0), 0.0)
            sh *= 2
        x = jnp.where(lanes < HEADS, g_all, ab)
        for piece in range(3):
            xb = x.astype(BF)
            gsplit_s[:, piece * HEAD_DIM:(piece + 1) * HEAD_DIM] = xb
            x = x - xb.astype(F32)

    sel_r = lax.broadcasted_iota(jnp.int32, (3 * HEAD_DIM, 2 * HEAD_DIM), 0) % HEAD_DIM
    sel_c = lax.broadcasted_iota(jnp.int32, (3 * HEAD_DIM, 2 * HEAD_DIM), 1)
    onehot = (sel_r == jnp.where(sel_c < HEAD_DIM, h, h + HEADS)).astype(BF)
    sel = _dot(gsplit_s[...], onehot)
    gcb = sel[:, :HEAD_DIM]
    beta = _sigmoid(sel[:, HEAD_DIM:])
    gc_s[...] = gcb
    beta_s[...] = beta
    other()

    qn = l2n(conv_silu(q_ref, cq_ref)) * (HEAD_DIM ** -0.5)
    qb_s[...] = qn.astype(BF)
    qd_s[...] = (qn * jnp.exp(gcb)).astype(BF)
    other()
    kn_s[...] = l2n(conv_silu(k_ref, ck_ref))
    other()
    vb_s[...] = (conv_silu(v_ref, cv_ref) * beta).astype(BF)
    other()

    ri = lax.broadcasted_iota(jnp.int32, (TILE, TILE), 0)
    ci = lax.broadcasted_iota(jnp.int32, (TILE, TILE), 1)
    incl = ri >= ci
    strict = ri > ci
    eye = (ri == ci).astype(F32)

    def prep_tiles(ts, tick):
        each = lambda f, *xs: [f(*a) for a in zip(*xs)]

        def each_mm(f, *xs):
            out = each(f, *xs)
            tick()
            return out

        sls = [pl.ds(t * TILE, TILE) for t in ts]
        k = [kn_s[sl, :] for sl in sls]
        gcb = [gc_s[sl, :] for sl in sls]
        bb = [beta_s[sl, :] for sl in sls]
        vb = [vb_s[sl, :] for sl in sls]
        qb = [qb_s[sl, :] for sl in sls]

        decay = each(lambda g: jnp.where(incl, jnp.exp(jnp.where(incl, g - g.T, 0.0)), 0.0), gcb)
        kb = each(lambda x: x.astype(BF), k)
        kk = each_mm(_dot_nt, kb, kb)
        m = each(lambda b_, kk_, d_: jnp.where(strict, b_ * kk_ * d_, 0.0), bb, kk, decay)

        base = 16
        base_blk = (ri // base) == (ci // base)
        mp = each(lambda m_: jnp.where(base_blk, m_, 0.0), m)
        p = each(lambda mp_: eye - mp_, mp)
        width = 2
        while width < base:
            mp = each_mm(_bdot, mp, mp)
            p = each_mm(lambda p_, mp_: p_ + _bdot(p_, mp_), p, mp)
            width *= 2
        size = base
        while size < TILE:
            off = ((ri // (2 * size)) == (ci // (2 * size))) & ((ri // size) != (ci // size))
            pc = each_mm(lambda p_, m_: _bdot(p_, jnp.where(off, m_, 0.0)), p, m)
            p = each_mm(lambda p_, pc_: p_ - _bdot(pc_, p_), p, pc)
            size *= 2

        rhs = each(lambda vb_, k_, b_, g_: jnp.concatenate([vb_, (k_ * (b_ * jnp.exp(g_))).astype(BF)], axis=1),
                   vb, k, bb, gcb)
        sol = each_mm(lambda p_, r_: _dot(p_.astype(BF), r_), p, rhs)
        qk = each_mm(lambda q_, kb_, d_: (_dot_nt(q_, kb_) * d_).astype(BF), qb, kb, decay)

        gl = [g[TILE - 1:TILE, :] for g in gcb]
        kdt = each(lambda k_, g_, l_: (k_ * jnp.exp(l_ - g_)).T.astype(BF), k, gcb, gl)
        uw = each(lambda s_: jnp.concatenate([s_[:, :HEAD_DIM], -s_[:, HEAD_DIM:]], axis=1).astype(BF), sol)
        ba = each_mm(_dot, kdt, uw)

        for i, t in enumerate(ts):
            u_s[sls[i], :] = sol[i][:, :HEAD_DIM]
            w_s[sls[i], :] = sol[i][:, HEAD_DIM:].astype(BF)
            qk_s[t] = qk[i]
            b_s[t] = ba[i][:, :HEAD_DIM]
            a_s[t] = ba[i][:, HEAD_DIM:].astype(BF)
            cd_s[t, 0:1, :] = jnp.exp(gl[i])

    depth = 4

    def scan_step(t, carry):
        state, sb, vn, oq, o = carry
        on = tuple(0 <= t - k < n_tiles for k in range(depth))
        new_sb, new_vn, new_oq, new_o = sb, vn, oq, o
        if on[0]:
            new_sb = state.astype(BF)
            nxt = _dot(a_s[t], new_sb)
        if on[1]:
            sl = pl.ds((t - 1) * TILE, TILE)
            new_vn = (u_s[sl, :] - _dot(w_s[sl, :], sb)).astype(BF)
            new_oq = _dot(qd_s[sl, :], sb)
        if on[2]:
            new_o = oq + _dot(qk_s[t - 2], vn)
        if on[0]:
            state = state * cd_s[t, 0:1, :] + nxt + b_s[t]
        if on[3]:
            sl = pl.ds((t - 3) * TILE, TILE)
            z = z_ref[sl, :].astype(F32)
            o_ref[sl, :] = (_rms(o, onorm_ref[...]) * (z * _sigmoid(z))).astype(BF)
        return state, new_sb, new_vn, new_oq, new_o

    scan = {"t": 0, "carry": (jnp.zeros((HEAD_DIM, HEAD_DIM), F32), None, None, None, None)}

    def scan_advance(prepared):
        t = scan["t"]
        if t < n_tiles + depth - 1 and (t < prepared or t >= n_tiles):
            scan["carry"] = scan_step(t, scan["carry"])
            scan["t"] = t + 1
            other()

    wave = -(-n_tiles // GDN_WAVES)
    for first in range(0, n_tiles, wave):
        chunks = list(range(first, min(first + wave, n_tiles)))
        ticks = {"n": 0}

        def tick(first=first, ticks=ticks):
            ticks["n"] += 1
            other()
            if ticks["n"] % GDN_SCAN_STRIDE == 0:
                scan_advance(first)

        prep_tiles(chunks, tick)
    while scan["t"] < n_tiles + depth - 1:
        scan_advance(n_tiles)


def _neg_abs(x):
    sign = jnp.uint32(0x80000000)
    return lax.bitcast_convert_type(lax.bitcast_convert_type(x, jnp.uint32) | sign, F32)


def _sb_trace(q_ref, k_ref, v_ref, r2_ref, o_ref, blk):
    s_len = q_ref.shape[0]
    n_sub = blk // SB_TK
    ri = lax.broadcasted_iota(jnp.int32, (blk, blk), 0)
    ci = lax.broadcasted_iota(jnp.int32, (blk, blk), 1)
    causal = ci < ri

    for qi in range(s_len // blk):
        qs = slice(qi * blk, (qi + 1) * blk)
        q = (q_ref[qs, :].astype(F32) * (HEAD_DIM ** -0.5 * LOG2E)).astype(BF)
        acc = jnp.zeros((blk, HEAD_DIM), F32)
        c = jnp.zeros((blk, SB_TK), F32)
        for jb in range(qi, -1, -1):
            masked = jb == qi
            ks_ = slice(jb * blk, (jb + 1) * blk)
            z2 = _dot_nt(q, k_ref[ks_, :])
            sp = jnp.maximum(z2, 0.0) + jnp.log2(1.0 + jnp.exp2(_neg_abs(z2)))
            if masked:
                sp = jnp.where(causal, sp, 0.0)
            hi, lo = _split(sp)
            yield
            wparts = [None] * n_sub
            for m in reversed(range(n_sub)):
                cs = slice(m * SB_TK, (m + 1) * SB_TK)
                sr = _dot(jnp.concatenate([hi[:, cs], lo[:, cs]], axis=1), r2_ref[...])
                w = jnp.exp2(z2[:, cs] + sr[:, :SB_TK] + c)
                if masked:
                    w = jnp.where(causal[:, cs], w, 0.0)
                wparts[m] = w.astype(BF)
                c = c + sr[:, SB_TK:]
                yield
            acc = acc + _dot(jnp.concatenate(wparts, axis=1), v_ref[ks_, :])
            yield
        o_ref[qs, :] = acc.astype(BF)


def _mixers_body(gq_ref, gk_ref, gv_ref, gz_ref, ab_ref, cq_ref, ck_ref, cv_ref, alog_ref, dt_ref, onorm_ref,
                 sq_ref, sk_ref, sv_ref, r2_ref, oa_ref, ob_ref, *gdn_scratch, blk):
    sb = _sb_trace(sq_ref, sk_ref, sv_ref, r2_ref, ob_ref, blk)
    _gdn_trace(gq_ref, gk_ref, gv_ref, gz_ref, ab_ref, cq_ref, ck_ref, cv_ref, alog_ref, dt_ref, onorm_ref,
               oa_ref, *gdn_scratch, other=lambda: next(sb, None))
    for _ in sb:
        pass


def _mixers(heads4, ab3, cw, alog_row, dt_row, onorm_row, blk):
    _, b, s, _ = heads4.shape
    n_tiles = s // TILE
    assert s % blk == 0 and blk % SB_TK == 0
    jr = jnp.arange(2 * SB_TK)[:, None] % SB_TK
    jc = jnp.arange(2 * SB_TK)[None, :]
    r2 = -jnp.where(jc < SB_TK, jr >= jc, True).astype(BF)

    def head_spec(base):
        return pl.BlockSpec((None, None, s, HEAD_DIM), lambda bi, hi: (base + hi, bi, 0, 0))

    def conv_spec(base):
        return pl.BlockSpec((GDN_CONV, HEAD_DIM), lambda bi, hi: (0, base + hi))

    row_spec = pl.BlockSpec((1, HEAD_DIM), lambda bi, hi: (0, 0))
    out_spec = lambda: pl.BlockSpec((None, s, HEAD_DIM), lambda bi, hi: (bi, 0, hi))
    seq_f32 = pltpu.VMEM((s, HEAD_DIM), F32)
    seq_bf = pltpu.VMEM((s, HEAD_DIM), BF)
    return pl.pallas_call(
        functools.partial(_mixers_body, blk=blk),
        grid=(b, HEADS),
        in_specs=[head_spec(_GQ), head_spec(_GK), head_spec(_GV), head_spec(_GZ),
                  pl.BlockSpec((None, s, HEAD_DIM), lambda bi, hi: (bi, 0, 0)),
                  conv_spec(0), conv_spec(HEADS), conv_spec(2 * HEADS),
                  row_spec, row_spec, row_spec,
                  head_spec(_SQ), head_spec(_SK), head_spec(_SV), _const_spec((2 * SB_TK, 2 * SB_TK))],
        out_specs=[out_spec(), out_spec()],
        out_shape=[jax.ShapeDtypeStruct((b, s, HEADS * HEAD_DIM), BF)] * 2,
        scratch_shapes=[pltpu.VMEM((s, 3 * HEAD_DIM), BF),
                        seq_bf, seq_bf, seq_f32, seq_bf,
                        seq_f32, seq_f32,
                        seq_f32, seq_bf,
                        pltpu.VMEM((n_tiles, TILE, TILE), BF),
                        pltpu.VMEM((n_tiles, HEAD_DIM, HEAD_DIM), BF),
                        pltpu.VMEM((n_tiles, HEAD_DIM, HEAD_DIM), F32),
                        pltpu.VMEM((n_tiles, 8, HEAD_DIM), F32)],
        compiler_params=_params("parallel", "arbitrary"),
        name="mixers",
    )(heads4, heads4, heads4, heads4, ab3, cw, cw, cw, alog_row, dt_row, onorm_row,
      heads4, heads4, heads4, r2)


def _merge_body(oa_ref, ob_ref, xn_ref, x_ref, wg_ref, wpa_ref, wpb_ref, wo_ref, h_ref):
    gates = _dot(xn_ref[...], wg_ref[...])
    pa = _dot(oa_ref[...], wpa_ref[...])
    pb = _dot(ob_ref[...], wpb_ref[...])
    merged = _sigmoid(gates[:, :D_MODEL]) * pa + _sigmoid(gates[:, D_MODEL:]) * pb
    h_ref[...] = x_ref[...] + _dot(merged.astype(BF), wo_ref[...])


def _merge_out(oa, ob, xn, x2, w_gates, w_pa, w_pb, w_o, tm):
    t = x2.shape[0]
    tile = lambda: pl.BlockSpec((tm, D_MODEL), lambda i: (i, 0))
    return pl.pallas_call(
        _merge_body,
        grid=(t // tm,),
        in_specs=[tile(), tile(), tile(), tile(),
                  _const_spec((D_MODEL, 2 * D_MODEL)), _const_spec((D_MODEL, D_MODEL)),
                  _const_spec((D_MODEL, D_MODEL)), _const_spec((D_MODEL, D_MODEL))],
        out_specs=tile(),
        out_shape=jax.ShapeDtypeStruct((t, D_MODEL), F32),
        compiler_params=_params("parallel"),
        name="merge_out",
    )(oa, ob, xn, x2, w_gates, w_pa, w_pb, w_o)


def _mem_kv_body(mem_ref, g_ref, w_ref, kn_ref, k_ref, v_ref):
    mn = _rms(mem_ref[...], g_ref[...]).astype(BF)
    kv = _dot(mn, w_ref[...])
    for hh in range(X_HEADS):
        cs = slice(hh * X_DH, (hh + 1) * X_DH)
        k_ref[:, cs] = _rms(kv[:, cs], kn_ref[...]).astype(BF)
    v_ref[...] = kv[:, D_MODEL:].astype(BF)


def _mem_kv(mem, g, w_xkv, k_norm):
    b, m, _ = mem.shape
    blk = lambda: pl.BlockSpec((None, m, D_MODEL), lambda bi: (bi, 0, 0))
    return pl.pallas_call(
        _mem_kv_body,
        grid=(b,),
        in_specs=[blk(), _const_spec((1, D_MODEL)), _const_spec((D_MODEL, 2 * D_MODEL)),
                  _const_spec((1, X_DH))],
        out_specs=[blk(), blk()],
        out_shape=[jax.ShapeDtypeStruct((b, m, D_MODEL), BF)] * 2,
        compiler_params=_params("parallel"),
        name="mem_kv",
    )(mem, g, w_xkv, k_norm)


def _xattn_body(h_ref, k_ref, v_ref, g_ref, qn_ref, wq_ref, wo_ref, o_ref, att_s):
    h = h_ref[...]
    q = _dot(_rms(h, g_ref[...]).astype(BF), wq_ref[...])
    for hh in range(X_HEADS):
        cs = slice(hh * X_DH, (hh + 1) * X_DH)
        qh = _rms(q[:, cs], qn_ref[...]).astype(BF)
        sc = _dot_nt(qh, k_ref[:, cs]) * (X_DH ** -0.5)
        e = jnp.exp(sc - jnp.max(sc, axis=-1, keepdims=True))
        p = e / jnp.sum(e, axis=-1, keepdims=True)
        att_s[:, cs] = _dot(p.astype(BF), v_ref[:, cs]).astype(BF)
    o_ref[...] = h + _dot(att_s[...], wo_ref[...])


def _xattn(h3, kx, vx, g, q_norm, w_q, w_o, tm):
    b, s, _ = h3.shape
    m = kx.shape[1]
    tile = lambda: pl.BlockSpec((None, tm, D_MODEL), lambda bi, i: (bi, i, 0))
    kv = lambda: pl.BlockSpec((None, m, D_MODEL), lambda bi, i: (bi, 0, 0))
    return pl.pallas_call(
        _xattn_body,
        grid=(b, s // tm),
        in_specs=[tile(), kv(), kv(), _const_spec((1, D_MODEL)), _const_spec((1, X_DH)),
                  _const_spec((D_MODEL, D_MODEL)), _const_spec((D_MODEL, D_MODEL))],
        out_specs=tile(),
        out_shape=jax.ShapeDtypeStruct((b, s, D_MODEL), F32),
        scratch_shapes=[pltpu.VMEM((tm, D_MODEL), BF)],
        compiler_params=_params("parallel", "parallel"),
        name="xattn",
    )(h3, kx, vx, g, q_norm, w_q, w_o)


def _ffn_body(h_ref, hp_ref, g_ref, wu_ref, cw_ref, wd_ref, o_ref, hx_s, act_s):
    h = h_ref[...]
    g = g_ref[...]
    first = pl.program_id(1) == 0
    hx_s[0:FFN_HALO, :] = jnp.where(first, 0.0, _rms(hp_ref[...], g)).astype(BF)
    hx_s[FFN_HALO:, :] = _rms(h, g).astype(BF)

    def up_conv(c0):
        a = _dot(hx_s[...], wu_ref[:, c0:c0 + FFN_FC])
        cw = cw_ref[:, c0:c0 + FFN_FC]
        y = a[FFN_HALO:, :] * cw[FFN_CONV - 1:FFN_CONV, :]
        for i in range(FFN_CONV - 1):
            sh = FFN_CONV - 1 - i
            y = y + pltpu.roll(a, sh, axis=0)[FFN_HALO:, :] * cw[i:i + 1, :]
        return y

    for f in range(D_FF // FFN_FC):
        ua = up_conv(f * FFN_FC)
        ug = up_conv(D_FF + f * FFN_FC)
        act_s[:, f * FFN_FC:(f + 1) * FFN_FC] = (ua * _sigmoid(ua) * ug).astype(BF)
    o_ref[...] = h + _dot(act_s[...], wd_ref[...])


def _ffn(h3, g, w_up, cw, w_down, tm):
    b, s, _ = h3.shape
    halo_blocks = tm // FFN_HALO
    return pl.pallas_call(
        _ffn_body,
        grid=(b, s // tm),
        in_specs=[
            pl.BlockSpec((None, tm, D_MODEL), lambda bi, i: (bi, i, 0)),
            pl.BlockSpec((None, FFN_HALO, D_MODEL),
                         lambda bi, i: (bi, jnp.maximum(i * halo_blocks - 1, 0), 0)),
            _const_spec((1, D_MODEL)),
            _const_spec((D_MODEL, 2 * D_FF)),
            _const_spec((FFN_CONV, 2 * D_FF)),
            _const_spec((D_FF, D_MODEL)),
        ],
        out_specs=pl.BlockSpec((None, tm, D_MODEL), lambda bi, i: (bi, i, 0)),
        out_shape=jax.ShapeDtypeStruct((b, s, D_MODEL), F32),
        scratch_shapes=[pltpu.VMEM((tm + FFN_HALO, D_MODEL), BF), pltpu.VMEM((tm, D_FF), BF)],
        compiler_params=_params("parallel", "arbitrary"),
        name="ffn",
    )(h3, h3, g, w_up, cw, w_down)


def _pick(n, pref):
    while n % pref:
        pref //= 2
    return pref


def _layer(h, mem, norm_mix, w_in, conv_gdn, a_log, dt_bias, gdn_out_norm, w_proj_gdn, w_proj_sb, w_out,
           norm_x, norm_mem, w_xq, w_xkv, xq_norm, xk_norm, w_xo, norm_ffn, w_up, conv_ffn, w_down):
    b, s, d = h.shape
    t = b * s
    assert d == D_MODEL and s % TILE == 0
    qkv_w = 3 * HEADS * HEAD_DIM
    ab_end = qkv_w + 2 * HEADS
    heads_end = ab_end + 4 * HEADS * HEAD_DIM

    w_heads = jnp.concatenate([w_in[:, :qkv_w], w_in[:, ab_end:heads_end]], axis=1).astype(BF)
    w_ab = jnp.pad(w_in[:, qkv_w:ab_end], ((0, 0), (0, HEAD_DIM - 2 * HEADS))).astype(BF)
    w_gates = w_in[:, heads_end:].astype(BF)
    pad_row = lambda v: jnp.pad(v, (0, HEAD_DIM - HEADS))[None, :]

    x2 = h.reshape(t, d)
    heads, xn, ab = _in_proj(x2, norm_mix[None, :], w_heads, w_ab, _pick(t, 2048))
    heads4 = heads.reshape(N_HEAD_BLOCKS, b, s, HEAD_DIM)
    o_a, o_b = _mixers(heads4, ab.reshape(b, s, HEAD_DIM), conv_gdn, pad_row(a_log), pad_row(dt_bias),
                       gdn_out_norm[None, :], _pick(s, SB_BLK))
    h1 = _merge_out(o_a.reshape(t, d), o_b.reshape(t, d), xn, x2, w_gates,
                    w_proj_gdn.astype(BF), w_proj_sb.astype(BF), w_out.astype(BF), _pick(t, 512))
    kx, vx = _mem_kv(mem, norm_mem[None, :], w_xkv.astype(BF), xk_norm[None, :])
    h2 = _xattn(h1.reshape(b, s, d), kx, vx, norm_x[None, :], xq_norm[None, :],
                w_xq.astype(BF), w_xo.astype(BF), _pick(s, 512))
    return _ffn(h2, norm_ffn[None, :], w_up.astype(BF), conv_ffn, w_down.astype(BF), _pick(s, 512))


def kernel(x, mem, norm_mix, w_in, conv_gdn, a_log, dt_bias, gdn_out_norm, w_proj_gdn, w_proj_sb, w_out,
           norm_x, norm_mem, w_xq, w_xkv, xq_norm, xk_norm, w_xo, norm_ffn, w_up, conv_ffn, w_down):
    h = x
    for l in range(norm_mix.shape[0]):
        h = _layer(h, mem, norm_mix[l], w_in[l], conv_gdn[l], a_log[l], dt_bias[l], gdn_out_norm[l],
                   w_proj_gdn[l], w_proj_sb[l], w_out[l], norm_x[l], norm_mem[l], w_xq[l], w_xkv[l],
                   xq_norm[l], xk_norm[l], w_xo[l], norm_ffn[l], w_up[l], conv_ffn[l], w_down[l])
    return h
```

```python
import functools

import jax
import jax.numpy as jnp
from jax import lax
from jax.experimental import pallas as pl
from jax.experimental.pallas import tpu as pltpu

F32 = jnp.float32
BF = jnp.bfloat16

D_MODEL = 1024
HEADS = 8
HEAD_DIM = 128
TILE = 128
GDN_CONV = 4
GDN_WAVES = 1
GDN_SCAN_STRIDE = 2
SB_TK = 128
SB_BLK = 512
X_HEADS = 4
X_DH = D_MODEL // X_HEADS
D_FF = 2816
FFN_CONV = 3
FFN_FC = 256
FFN_HALO = 16
EPS = 1e-6
LOG2E = 1.4426950408889634
VMEM_LIMIT = 56 * 1024 * 1024

_GQ, _GK, _GV, _GZ, _SQ, _SK, _SV = (i * HEADS for i in range(7))
N_HEAD_BLOCKS = 7 * HEADS


def _dot(a, b):
    return jnp.dot(a, b, preferred_element_type=F32)


def _dot_nt(a, b):
    return lax.dot_general(a, b, (((1,), (1,)), ((), ())), preferred_element_type=F32)


def _bdot(a, b):
    return _dot(a.astype(BF), b.astype(BF))


def _rms(x, g):
    return x * lax.rsqrt(jnp.mean(x * x, axis=-1, keepdims=True) + EPS) * g


def _softplus(x):
    return jnp.maximum(x, 0.0) + jnp.log1p(jnp.exp(-jnp.abs(x)))


def _sigmoid(x):
    return 1.0 / (1.0 + jnp.exp(-x))


def _params(*sem):
    return pltpu.CompilerParams(dimension_semantics=sem, vmem_limit_bytes=VMEM_LIMIT)


def _const_spec(shape):
    n = len(shape)
    return pl.BlockSpec(shape, lambda *_: (0,) * n, pipeline_mode=pl.Buffered(1))


def _in_proj_body(x_ref, g_ref, w_ref, wab_ref, heads_ref, xn_ref, ab_ref):
    @pl.when(pl.program_id(1) == 0)
    def _():
        xn = _rms(x_ref[...], g_ref[...]).astype(BF)
        xn_ref[...] = xn
        ab_ref[...] = _dot(xn, wab_ref[...])

    res = _dot(xn_ref[...], w_ref[...])
    for c in range(HEADS):
        heads_ref[c] = res[:, c * HEAD_DIM:(c + 1) * HEAD_DIM].astype(BF)


def _in_proj(x2, g, w_heads, w_ab, tm):
    t = x2.shape[0]
    nj = N_HEAD_BLOCKS // HEADS
    wn = HEADS * HEAD_DIM
    return pl.pallas_call(
        _in_proj_body,
        grid=(t // tm, nj),
        in_specs=[
            pl.BlockSpec((tm, D_MODEL), lambda i, j: (i, 0)),
            _const_spec((1, D_MODEL)),
            pl.BlockSpec((D_MODEL, wn), lambda i, j: (0, j)),
            _const_spec((D_MODEL, HEAD_DIM)),
        ],
        out_specs=[
            pl.BlockSpec((HEADS, tm, HEAD_DIM), lambda i, j: (j, i, 0)),
            pl.BlockSpec((tm, D_MODEL), lambda i, j: (i, 0)),
            pl.BlockSpec((tm, HEAD_DIM), lambda i, j: (i, 0)),
        ],
        out_shape=[
            jax.ShapeDtypeStruct((N_HEAD_BLOCKS, t, HEAD_DIM), BF),
            jax.ShapeDtypeStruct((t, D_MODEL), BF),
            jax.ShapeDtypeStruct((t, HEAD_DIM), F32),
        ],
        compiler_params=_params("parallel", "arbitrary"),
        name="in_proj",
    )(x2, g, w_heads, w_ab)


def _gdn_trace(q_ref, k_ref, v_ref, z_ref, ab_ref, cq_ref, ck_ref, cv_ref, alog_ref, dt_ref, onorm_ref,
               o_ref,
               gsplit_s, qb_s, qd_s, kn_s, vb_s, gc_s, beta_s, u_s, w_s, qk_s, a_s, b_s, cd_s, other):
    s_len = q_ref.shape[0]
    n_tiles = s_len // TILE
    h = pl.program_id(1)

    def conv_silu(x_ref, cw_ref):
        x = x_ref[...].astype(F32)
        cw = cw_ref[...]
        xz = jnp.concatenate([jnp.zeros((8, HEAD_DIM), F32), x], axis=0)
        acc = x * cw[GDN_CONV - 1:GDN_CONV, :]
        for i in range(GDN_CONV - 1):
            acc = acc + pltpu.roll(xz, GDN_CONV - 1 - i, axis=0)[8:, :] * cw[i:i + 1, :]
        return acc * _sigmoid(acc)

    def l2n(x):
        return x * lax.rsqrt(jnp.sum(x * x, axis=-1, keepdims=True) + EPS)

    @pl.when(h == 0)
    def _():
        rows = lax.broadcasted_iota(jnp.int32, (s_len, HEAD_DIM), 0)
        lanes = lax.broadcasted_iota(jnp.int32, (s_len, HEAD_DIM), 1)
        ab = ab_ref[...]
        g_all = -jnp.exp(alog_ref[...]) * _softplus(ab + dt_ref[...])
        pos = rows % TILE
        sh = 1
        while sh < TILE:
            g_all = g_all + jnp.where(pos >= sh, pltpu.roll(g_all, sh, axis=0), 0.0)
            sh *= 2
        x = jnp.where(lanes < HEADS, g_all, ab)
        for piece in range(3):
            xb = x.astype(BF)
            gsplit_s[:, piece * HEAD_DIM:(piece + 1) * HEAD_DIM] = xb
            x = x - xb.astype(F32)

    sel_r = lax.broadcasted_iota(jnp.int32, (3 * HEAD_DIM, 2 * HEAD_DIM), 0) % HEAD_DIM
    sel_c = lax.broadcasted_iota(jnp.int32, (3 * HEAD_DIM, 2 * HEAD_DIM), 1)
    onehot = (sel_r == jnp.where(sel_c < HEAD_DIM, h, h + HEADS)).astype(BF)
    sel = _dot(gsplit_s[...], onehot)
    gcb = sel[:, :HEAD_DIM]
    beta = _sigmoid(sel[:, HEAD_DIM:])
    gc_s[...] = gcb
    beta_s[...] = beta
    other()

    qn = l2n(conv_silu(q_ref, cq_ref)) * (HEAD_DIM ** -0.5)
    qb_s[...] = qn.astype(BF)
    qd_s[...] = (qn * jnp.exp(gcb)).astype(BF)
    other()
    kn_s[...] = l2n(conv_silu(k_ref, ck_ref))
    other()
    vb_s[...] = (conv_silu(v_ref, cv_ref) * beta).astype(BF)
    other()

    ri = lax.broadcasted_iota(jnp.int32, (TILE, TILE), 0)
    ci = lax.broadcasted_iota(jnp.int32, (TILE, TILE), 1)
    incl = ri >= ci
    strict = ri > ci
    eye = (ri == ci).astype(F32)

    def prep_tiles(ts, tick):
        each = lambda f, *xs: [f(*a) for a in zip(*xs)]

        def each_mm(f, *xs):
            out = each(f, *xs)
            tick()
            return out

        sls = [pl.ds(t * TILE, TILE) for t in ts]
        k = [kn_s[sl, :] for sl in sls]
        gcb = [gc_s[sl, :] for sl in sls]
        bb = [beta_s[sl, :] for sl in sls]
        vb = [vb_s[sl, :] for sl in sls]
        qb = [qb_s[sl, :] for sl in sls]

        decay = each(lambda g: jnp.where(incl, jnp.exp(jnp.where(incl, g - g.T, 0.0)), 0.0), gcb)
        kb = each(lambda x: x.astype(BF), k)
        kk = each_mm(_dot_nt, kb, kb)
        m = each(lambda b_, kk_, d_: jnp.where(strict, b_ * kk_ * d_, 0.0), bb, kk, decay)

        base = 16
        base_blk = (ri // base) == (ci // base)
        mp = each(lambda m_: jnp.where(base_blk, m_, 0.0), m)
        p = each(lambda mp_: eye - mp_, mp)
        width = 2
        while width < base:
            mp = each_mm(_bdot, mp, mp)
            p = each_mm(lambda p_, mp_: p_ + _bdot(p_, mp_), p, mp)
            width *= 2
        size = base
        while size < TILE:
            off = ((ri // (2 * size)) == (ci // (2 * size))) & ((ri // size) != (ci // size))
            pc = each_mm(lambda p_, m_: _bdot(p_, jnp.where(off, m_, 0.0)), p, m)
            p = each_mm(lambda p_, pc_: p_ - _bdot(pc_, p_), p, pc)
            size *= 2

        rhs = each(lambda vb_, k_, b_, g_: jnp.concatenate([vb_, (k_ * (b_ * jnp.exp(g_))).astype(BF)], axis=1),
                   vb, k, bb, gcb)
        sol = each_mm(lambda p_, r_: _dot(p_.astype(BF), r_), p, rhs)
        qk = each_mm(lambda q_, kb_, d_: (_dot_nt(q_, kb_) * d_).astype(BF), qb, kb, decay)

        gl = [g[TILE - 1:TILE, :] for g in gcb]
        kdt = each(lambda k_, g_, l_: (k_ * jnp.exp(l_ - g_)).T.astype(BF), k, gcb, gl)
        uw = each(lambda s_: jnp.concatenate([s_[:, :HEAD_DIM], -s_[:, HEAD_DIM:]], axis=1).astype(BF), sol)
        ba = each_mm(_dot, kdt, uw)

        for i, t in enumerate(ts):
            u_s[sls[i], :] = sol[i][:, :HEAD_DIM]
            w_s[sls[i], :] = sol[i][:, HEAD_DIM:].astype(BF)
            qk_s[t] = qk[i]
            b_s[t] = ba[i][:, :HEAD_DIM]
            a_s[t] = ba[i][:, HEAD_DIM:].astype(BF)
            cd_s[t, 0:1, :] = jnp.exp(gl[i])

    depth = 4

    def scan_step(t, carry):
        state, sb, vn, oq, o = carry
        on = tuple(0 <= t - k < n_tiles for k in range(depth))
        new_sb, new_vn, new_oq, new_o = sb, vn, oq, o
        if on[0]:
            new_sb = state.astype(BF)
            nxt = _dot(a_s[t], new_sb)
        if on[1]:
            sl = pl.ds((t - 1) * TILE, TILE)
            new_vn = (u_s[sl, :] - _dot(w_s[sl, :], sb)).astype(BF)
            new_oq = _dot(qd_s[sl, :], sb)
        if on[2]:
            new_o = oq + _dot(qk_s[t - 2], vn)
        if on[0]:
            state = state * cd_s[t, 0:1, :] + nxt + b_s[t]
        if on[3]:
            sl = pl.ds((t - 3) * TILE, TILE)
            z = z_ref[sl, :].astype(F32)
            o_ref[sl, :] = (_rms(o, onorm_ref[...]) * (z * _sigmoid(z))).astype(BF)
        return state, new_sb, new_vn, new_oq, new_o

    scan = {"t": 0, "carry": (jnp.zeros((HEAD_DIM, HEAD_DIM), F32), None, None, None, None)}

    def scan_advance(prepared):
        t = scan["t"]
        if t < n_tiles + depth - 1 and (t < prepared or t >= n_tiles):
            scan["carry"] = scan_step(t, scan["carry"])
            scan["t"] = t + 1
            other()

    wave = -(-n_tiles // GDN_WAVES)
    for first in range(0, n_tiles, wave):
        chunks = list(range(first, min(first + wave, n_tiles)))
        ticks = {"n": 0}

        def tick(first=first, ticks=ticks):
            ticks["n"] += 1
            other()
            if ticks["n"] % GDN_SCAN_STRIDE == 0:
                scan_advance(first)

        prep_tiles(chunks, tick)
    while scan["t"] < n_tiles + depth - 1:
        scan_advance(n_tiles)


def _neg_abs(x):
    sign = jnp.uint32(0x80000000)
    return lax.bitcast_convert_type(lax.bitcast_convert_type(x, jnp.uint32) | sign, F32)


def _sb_trace(q_ref, k_ref, v_ref, r2_ref, o_ref, blk):
    s_len = q_ref.shape[0]
    n_sub = blk // SB_TK
    ri = lax.broadcasted_iota(jnp.int32, (blk, blk), 0)
    ci = lax.broadcasted_iota(jnp.int32, (blk, blk), 1)
    causal = ci < ri

    for qi in range(s_len // blk):
        qs = slice(qi * blk, (qi + 1) * blk)
        q = (q_ref[qs, :].astype(F32) * (HEAD_DIM ** -0.5 * LOG2E)).astype(BF)
        acc = jnp.zeros((blk, HEAD_DIM), F32)
        c = jnp.zeros((blk, SB_TK), F32)
        for jb in range(qi, -1, -1):
            masked = jb == qi
            ks_ = slice(jb * blk, (jb + 1) * blk)
            z2 = _dot_nt(q, k_ref[ks_, :])
            sp = jnp.maximum(z2, 0.0) + jnp.log2(1.0 + jnp.exp2(_neg_abs(z2)))
            if masked:
                sp = jnp.where(causal, sp, 0.0)
            spb = sp.astype(BF)
            yield
            wparts = [None] * n_sub
            for m in reversed(range(n_sub)):
                cs = slice(m * SB_TK, (m + 1) * SB_TK)
                sr = _dot(spb[:, cs], r2_ref[...])
                w = jnp.exp2(z2[:, cs] + sr[:, :SB_TK] + c)
                if masked:
                    w = jnp.where(causal[:, cs], w, 0.0)
                wparts[m] = w.astype(BF)
                c = c + sr[:, SB_TK:]
                yield
            acc = acc + _dot(jnp.concatenate(wparts, axis=1), v_ref[ks_, :])
            yield
        o_ref[qs, :] = acc.astype(BF)


def _mixers_body(gq_ref, gk_ref, gv_ref, gz_ref, ab_ref, cq_ref, ck_ref, cv_ref, alog_ref, dt_ref, onorm_ref,
                 sq_ref, sk_ref, sv_ref, r2_ref, oa_ref, ob_ref, *gdn_scratch, blk):
    sb = _sb_trace(sq_ref, sk_ref, sv_ref, r2_ref, ob_ref, blk)
    _gdn_trace(gq_ref, gk_ref, gv_ref, gz_ref, ab_ref, cq_ref, ck_ref, cv_ref, alog_ref, dt_ref, onorm_ref,
               oa_ref, *gdn_scratch, other=lambda: next(sb, None))
    for _ in sb:
        pass


def _mixers(heads4, ab3, cw, alog_row, dt_row, onorm_row, blk):
    _, b, s, _ = heads4.shape
    n_tiles = s // TILE
    assert s % blk == 0 and blk % SB_TK == 0
    jr = jnp.arange(SB_TK)[:, None]
    jc = jnp.arange(2 * SB_TK)[None, :]
    r2 = -jnp.where(jc < SB_TK, jr >= jc, True).astype(BF)

    def head_spec(base):
        return pl.BlockSpec((None, None, s, HEAD_DIM), lambda bi, hi: (base + hi, bi, 0, 0))

    def conv_spec(base):
        return pl.BlockSpec((GDN_CONV, HEAD_DIM), lambda bi, hi: (0, base + hi))

    row_spec = pl.BlockSpec((1, HEAD_DIM), lambda bi, hi: (0, 0))
    out_spec = lambda: pl.BlockSpec((None, s, HEAD_DIM), lambda bi, hi: (bi, 0, hi))
    seq_f32 = pltpu.VMEM((s, HEAD_DIM), F32)
    seq_bf = pltpu.VMEM((s, HEAD_DIM), BF)
    return pl.pallas_call(
        functools.partial(_mixers_body, blk=blk),
        grid=(b, HEADS),
        in_specs=[head_spec(_GQ), head_spec(_GK), head_spec(_GV), head_spec(_GZ),
                  pl.BlockSpec((None, s, HEAD_DIM), lambda bi, hi: (bi, 0, 0)),
                  conv_spec(0), conv_spec(HEADS), conv_spec(2 * HEADS),
                  row_spec, row_spec, row_spec,
                  head_spec(_SQ), head_spec(_SK), head_spec(_SV), _const_spec((SB_TK, 2 * SB_TK))],
        out_specs=[out_spec(), out_spec()],
        out_shape=[jax.ShapeDtypeStruct((b, s, HEADS * HEAD_DIM), BF)] * 2,
        scratch_shapes=[pltpu.VMEM((s, 3 * HEAD_DIM), BF),
                        seq_bf, seq_bf, seq_f32, seq_bf,
                        seq_f32, seq_f32,
                        seq_f32, seq_bf,
                        pltpu.VMEM((n_tiles, TILE, TILE), BF),
                        pltpu.VMEM((n_tiles, HEAD_DIM, HEAD_DIM), BF),
                        pltpu.VMEM((n_tiles, HEAD_DIM, HEAD_DIM), F32),
                        pltpu.VMEM((n_tiles, 8, HEAD_DIM), F32)],
        compiler_params=_params("parallel", "arbitrary"),
        name="mixers",
    )(heads4, heads4, heads4, heads4, ab3, cw, cw, cw, alog_row, dt_row, onorm_row,
      heads4, heads4, heads4, r2)


def _merge_body(oa_ref, ob_ref, xn_ref, x_ref, wg_ref, wpa_ref, wpb_ref, wo_ref, h_ref):
    gates = _dot(xn_ref[...], wg_ref[...])
    pa = _dot(oa_ref[...], wpa_ref[...])
    pb = _dot(ob_ref[...], wpb_ref[...])
    merged = _sigmoid(gates[:, :D_MODEL]) * pa + _sigmoid(gates[:, D_MODEL:]) * pb
    h_ref[...] = x_ref[...] + _dot(merged.astype(BF), wo_ref[...])


def _merge_out(oa, ob, xn, x2, w_gates, w_pa, w_pb, w_o, tm):
    t = x2.shape[0]
    tile = lambda: pl.BlockSpec((tm, D_MODEL), lambda i: (i, 0))
    return pl.pallas_call(
        _merge_body,
        grid=(t // tm,),
        in_specs=[tile(), tile(), tile(), tile(),
                  _const_spec((D_MODEL, 2 * D_MODEL)), _const_spec((D_MODEL, D_MODEL)),
                  _const_spec((D_MODEL, D_MODEL)), _const_spec((D_MODEL, D_MODEL))],
        out_specs=tile(),
        out_shape=jax.ShapeDtypeStruct((t, D_MODEL), F32),
        compiler_params=_params("parallel"),
        name="merge_out",
    )(oa, ob, xn, x2, w_gates, w_pa, w_pb, w_o)


def _mem_kv_body(mem_ref, g_ref, w_ref, kn_ref, k_ref, v_ref):
    mn = _rms(mem_ref[...], g_ref[...]).astype(BF)
    kv = _dot(mn, w_ref[...])
    for hh in range(X_HEADS):
        cs = slice(hh * X_DH, (hh + 1) * X_DH)
        k_ref[:, cs] = _rms(kv[:, cs], kn_ref[...]).astype(BF)
    v_ref[...] = kv[:, D_MODEL:].astype(BF)


def _mem_kv(mem, g, w_xkv, k_norm):
    b, m, _ = mem.shape
    blk = lambda: pl.BlockSpec((None, m, D_MODEL), lambda bi: (bi, 0, 0))
    return pl.pallas_call(
        _mem_kv_body,
        grid=(b,),
        in_specs=[blk(), _const_spec((1, D_MODEL)), _const_spec((D_MODEL, 2 * D_MODEL)),
                  _const_spec((1, X_DH))],
        out_specs=[blk(), blk()],
        out_shape=[jax.ShapeDtypeStruct((b, m, D_MODEL), BF)] * 2,
        compiler_params=_params("parallel"),
        name="mem_kv",
    )(mem, g, w_xkv, k_norm)


def _xattn_body(h_ref, k_ref, v_ref, g_ref, qn_ref, wq_ref, wo_ref, o_ref, att_s):
    h = h_ref[...]
    q = _dot(_rms(h, g_ref[...]).astype(BF), wq_ref[...])
    for hh in range(X_HEADS):
        cs = slice(hh * X_DH, (hh + 1) * X_DH)
        qh = _rms(q[:, cs], qn_ref[...]).astype(BF)
        sc = _dot_nt(qh, k_ref[:, cs]) * (X_DH ** -0.5)
        e = jnp.exp(sc - jnp.max(sc, axis=-1, keepdims=True))
        p = e / jnp.sum(e, axis=-1, keepdims=True)
        att_s[:, cs] = _dot(p.astype(BF), v_ref[:, cs]).astype(BF)
    o_ref[...] = h + _dot(att_s[...], wo_ref[...])


def _xattn(h3, kx, vx, g, q_norm, w_q, w_o, tm):
    b, s, _ = h3.shape
    m = kx.shape[1]
    tile = lambda: pl.BlockSpec((None, tm, D_MODEL), lambda bi, i: (bi, i, 0))
    kv = lambda: pl.BlockSpec((None, m, D_MODEL), lambda bi, i: (bi, 0, 0))
    return pl.pallas_call(
        _xattn_body,
        grid=(b, s // tm),
        in_specs=[tile(), kv(), kv(), _const_spec((1, D_MODEL)), _const_spec((1, X_DH)),
                  _const_spec((D_MODEL, D_MODEL)), _const_spec((D_MODEL, D_MODEL))],
        out_specs=tile(),
        out_shape=jax.ShapeDtypeStruct((b, s, D_MODEL), F32),
        scratch_shapes=[pltpu.VMEM((tm, D_MODEL), BF)],
        compiler_params=_params("parallel", "parallel"),
        name="xattn",
    )(h3, kx, vx, g, q_norm, w_q, w_o)


def _ffn_body(h_ref, hp_ref, g_ref, wu_ref, cw_ref, wd_ref, o_ref, hx_s, act_s):
    h = h_ref[...]
    g = g_ref[...]
    first = pl.program_id(1) == 0
    hx_s[0:FFN_HALO, :] = jnp.where(first, 0.0, _rms(hp_ref[...], g)).astype(BF)
    hx_s[FFN_HALO:, :] = _rms(h, g).astype(BF)

    def up_conv(c0):
        a = _dot(hx_s[...], wu_ref[:, c0:c0 + FFN_FC])
        cw = cw_ref[:, c0:c0 + FFN_FC]
        y = a[FFN_HALO:, :] * cw[FFN_CONV - 1:FFN_CONV, :]
        for i in range(FFN_CONV - 1):
            sh = FFN_CONV - 1 - i
            y = y + pltpu.roll(a, sh, axis=0)[FFN_HALO:, :] * cw[i:i + 1, :]
        return y

    for f in range(D_FF // FFN_FC):
        ua = up_conv(f * FFN_FC)
        ug = up_conv(D_FF + f * FFN_FC)
        act_s[:, f * FFN_FC:(f + 1) * FFN_FC] = (ua * _sigmoid(ua) * ug).astype(BF)
    o_ref[...] = h + _dot(act_s[...], wd_ref[...])


def _ffn(h3, g, w_up, cw, w_down, tm):
    b, s, _ = h3.shape
    halo_blocks = tm // FFN_HALO
    return pl.pallas_call(
        _ffn_body,
        grid=(b, s // tm),
        in_specs=[
            pl.BlockSpec((None, tm, D_MODEL), lambda bi, i: (bi, i, 0)),
            pl.BlockSpec((None, FFN_HALO, D_MODEL),
                         lambda bi, i: (bi, jnp.maximum(i * halo_blocks - 1, 0), 0)),
            _const_spec((1, D_MODEL)),
            _const_spec((D_MODEL, 2 * D_FF)),
            _const_spec((FFN_CONV, 2 * D_FF)),
            _const_spec((D_FF, D_MODEL)),
        ],
        out_specs=pl.BlockSpec((None, tm, D_MODEL), lambda bi, i: (bi, i, 0)),
        out_shape=jax.ShapeDtypeStruct((b, s, D_MODEL), F32),
        scratch_shapes=[pltpu.VMEM((tm + FFN_HALO, D_MODEL), BF), pltpu.VMEM((tm, D_FF), BF)],
        compiler_params=_params("parallel", "arbitrary"),
        name="ffn",
    )(h3, h3, g, w_up, cw, w_down)


def _pick(n, pref):
    while n % pref:
        pref //= 2
    return pref


def _layer(h, mem, norm_mix, w_in, conv_gdn, a_log, dt_bias, gdn_out_norm, w_proj_gdn, w_proj_sb, w_out,
           norm_x, norm_mem, w_xq, w_xkv, xq_norm, xk_norm, w_xo, norm_ffn, w_up, conv_ffn, w_down):
    b, s, d = h.shape
    t = b * s
    assert d == D_MODEL and s % TILE == 0
    qkv_w = 3 * HEADS * HEAD_DIM
    ab_end = qkv_w + 2 * HEADS
    heads_end = ab_end + 4 * HEADS * HEAD_DIM

    w_heads = jnp.concatenate([w_in[:, :qkv_w], w_in[:, ab_end:heads_end]], axis=1).astype(BF)
    w_ab = jnp.pad(w_in[:, qkv_w:ab_end], ((0, 0), (0, HEAD_DIM - 2 * HEADS))).astype(BF)
    w_gates = w_in[:, heads_end:].astype(BF)
    pad_row = lambda v: jnp.pad(v, (0, HEAD_DIM - HEADS))[None, :]

    x2 = h.reshape(t, d)
    heads, xn, ab = _in_proj(x2, norm_mix[None, :], w_heads, w_ab, _pick(t, 2048))
    heads4 = heads.reshape(N_HEAD_BLOCKS, b, s, HEAD_DIM)
    o_a, o_b = _mixers(heads4, ab.reshape(b, s, HEAD_DIM), conv_gdn, pad_row(a_log), pad_row(dt_bias),
                       gdn_out_norm[None, :], _pick(s, SB_BLK))
    h1 = _merge_out(o_a.reshape(t, d), o_b.reshape(t, d), xn, x2, w_gates,
                    w_proj_gdn.astype(BF), w_proj_sb.astype(BF), w_out.astype(BF), _pick(t, 512))
    kx, vx = _mem_kv(mem, norm_mem[None, :], w_xkv.astype(BF), xk_norm[None, :])
    h2 = _xattn(h1.reshape(b, s, d), kx, vx, norm_x[None, :], xq_norm[None, :],
                w_xq.astype(BF), w_xo.astype(BF), _pick(s, 512))
    return _ffn(h2, norm_ffn[None, :], w_up.astype(BF), conv_ffn, w_down.astype(BF), _pick(s, 512))


def kernel(x, mem, norm_mix, w_in, conv_gdn, a_log, dt_bias, gdn_out_norm, w_proj_gdn, w_proj_sb, w_out,
           norm_x, norm_mem, w_xq, w_xkv, xq_norm, xk_norm, w_xo, norm_ffn, w_up, conv_ffn, w_down):
    h = x
    for l in range(norm_mix.shape[0]):
        h = _layer(h, mem, norm_mix[l], w_in[l], conv_gdn[l], a_log[l], dt_bias[l], gdn_out_norm[l],
                   w_proj_gdn[l], w_proj_sb[l], w_out[l], norm_x[l], norm_mem[l], w_xq[l], w_xkv[l],
                   xq_norm[l], xk_norm[l], w_xo[l], norm_ffn[l], w_up[l], conv_ffn[l], w_down[l])
    return h
```
